```python
import math
import jax, jax.numpy as jnp
from jax import lax
import numpy as np

D_MODEL = 2048
BATCH = 8
SEQ = 2048
DEPTH = 2

GRID_W = 64
CTX_LEN = 256
HEAD_DIM = 128
N_MIX_HEADS = D_MODEL // HEAD_DIM
A_HEADS = N_MIX_HEADS // 2
A_KV_HEADS = 2
B_GROUPS = N_MIX_HEADS // 4
C_HEADS = N_MIX_HEADS - A_HEADS - B_GROUPS
A_WIDTH = A_HEADS * HEAD_DIM
A_KV_WIDTH = A_KV_HEADS * HEAD_DIM
HY_WIDTH = B_GROUPS * HEAD_DIM
C_WIDTH = C_HEADS * HEAD_DIM
D_MIX = A_WIDTH + HY_WIDTH + C_WIDTH
PROJ_SIZES = (A_WIDTH, A_KV_WIDTH, A_KV_WIDTH, 3 * HY_WIDTH, C_WIDTH, C_WIDTH, C_WIDTH)
D_IN = A_WIDTH + 2 * A_KV_WIDTH + 3 * HY_WIDTH + 3 * C_WIDTH
Q_BLOCK = 128
ROPE_THETA = 10000.0
HY_ORDER = 2
HY_SHORT = 3
HY_BANDS = 16
HY_POS_DIM = 1 + 2 * HY_BANDS
HY_FFN = 64
HY_DECAY_TARGET = 1e-2
HY_DECAY_SHORT_PCT = 0.3
HY_DECAY_LONG_PCT = 1.5
HY_DECAY_SHIFT = 0.05
WIN_R = 8
WIN_C = 16
D_FF = 4 * D_MODEL
DN_ALPHA = (2 * DEPTH) ** 0.25
DN_BETA = (8 * DEPTH) ** -0.25
EPS = 1e-6
NEG_INF = -1e30

kernel_name = 'hymba_gqa_hyena_natten_deepnorm_dit'


def _rms(x):
    xf = x.astype(jnp.float32)
    return xf * lax.rsqrt(jnp.mean(xf * xf, axis=-1, keepdims=True) + EPS)


def rms_norm(x, g):
    return (_rms(x) * g.astype(jnp.float32)).astype(x.dtype)


def layer_norm(x, g, b):
    xf = x.astype(jnp.float32)
    mu = jnp.mean(xf, axis=-1, keepdims=True)
    var = jnp.mean(jnp.square(xf - mu), axis=-1, keepdims=True)
    y = (xf - mu) * lax.rsqrt(var + EPS) * g.astype(jnp.float32) + b.astype(jnp.float32)
    return y.astype(x.dtype)


def to_heads(t, n_heads):
    return t.reshape(t.shape[:-1] + (n_heads, HEAD_DIM))


def split_projection(p):
    cuts, acc = [], 0
    for s in PROJ_SIZES[:-1]:
        acc += s
        cuts.append(acc)
    return jnp.split(p, cuts, axis=-1)


def axial_rope_tables(n_tokens):
    t = jnp.arange(n_tokens)
    row = (t // GRID_W).astype(jnp.float32)
    col = (t % GRID_W).astype(jnp.float32)
    n_pairs_axis = HEAD_DIM // 4
    inv_freq = ROPE_THETA ** (-jnp.arange(n_pairs_axis, dtype=jnp.float32) / n_pairs_axis)
    ang = jnp.concatenate([row[:, None] * inv_freq[None, :], col[:, None] * inv_freq[None, :]], axis=-1)
    return jnp.cos(ang), jnp.sin(ang)


def apply_rope(x, cos, sin):
    xf = x.astype(jnp.float32).reshape(x.shape[:-1] + (HEAD_DIM // 2, 2))
    x0, x1 = xf[..., 0], xf[..., 1]
    cs = cos[None, :, None, :]
    sn = sin[None, :, None, :]
    out = jnp.stack([x0 * cs - x1 * sn, x0 * sn + x1 * cs], axis=-1)
    return out.reshape(x.shape).astype(x.dtype)


def dense_attention(q, k, v):
    b, n, nh, hd = q.shape
    hkv = k.shape[2]
    qg = q.reshape(b, n, hkv, nh // hkv, hd)
    s = jnp.einsum('bqhgd,bkhd->bhgqk', qg, k).astype(jnp.float32) * (hd ** -0.5)
    p = jax.nn.softmax(s, axis=-1).astype(v.dtype)
    return jnp.einsum('bhgqk,bkhd->bqhgd', p, v).reshape(b, n, nh * hd)


def gqa_latent_attention(q, k, v, k_ctx, v_ctx):
    b, n = q.shape[:2]
    grp = A_HEADS // A_KV_HEADS
    k_all = jnp.concatenate([k, k_ctx], axis=1)
    v_all = jnp.concatenate([v, v_ctx], axis=1)
    qb = q.reshape(b, n // Q_BLOCK, Q_BLOCK, A_KV_HEADS, grp, HEAD_DIM).transpose(1, 0, 2, 3, 4, 5)
    scale = HEAD_DIM ** -0.5

    def one_block(q_blk):
        s = jnp.einsum('bqhgd,bkhd->bhgqk', q_blk, k_all).astype(jnp.float32) * scale
        p = jax.nn.softmax(s, axis=-1).astype(v_all.dtype)
        return jnp.einsum('bhgqk,bkhd->bqhgd', p, v_all)

    o = lax.map(one_block, qb)
    return o.transpose(1, 0, 2, 3, 4, 5).reshape(b, n, A_WIDTH)


def hyena_filter_spectrum(n, w1, b1, freq, w2, b2, w3):
    f32 = jnp.float32
    t = jnp.linspace(0.0, 1.0, n, dtype=f32)
    bands = jnp.arange(1, HY_BANDS + 1, dtype=f32)
    ang = 2.0 * math.pi * t[:, None] * bands[None, :]
    feat = jnp.concatenate([t[:, None], jnp.cos(ang), jnp.sin(ang)], axis=-1)
    fr = freq.astype(f32)
    hdn = jnp.sin(fr * (feat @ w1.astype(f32) + b1.astype(f32)))
    hdn = jnp.sin(fr * (hdn @ w2.astype(f32) + b2.astype(f32)))
    filt = (hdn @ w3.astype(f32)).reshape(n, HY_ORDER, 2, HY_WIDTH)
    max_decay = math.log(HY_DECAY_TARGET) / HY_DECAY_SHORT_PCT
    min_decay = math.log(HY_DECAY_TARGET) / HY_DECAY_LONG_PCT
    deltas = jnp.abs(jnp.linspace(min_decay, max_decay, HY_WIDTH, dtype=f32))
    window = jnp.exp(-t[:, None] * deltas[None, :]) + HY_DECAY_SHIFT
    filt = filt * window[:, None, None, :]
    fwd = filt[:, :, 0]
    bwd = filt[:, :, 1]
    circ = jnp.concatenate([fwd, jnp.zeros_like(fwd[:1]), bwd[1:][::-1]], axis=0)
    circ = circ / (jnp.sum(jnp.abs(circ), axis=0, keepdims=True) + EPS)
    return jnp.fft.rfft(circ, axis=0)


def hyena_operator(proj, short_w, short_b, spec, skip):
    n = proj.shape[1]
    half = HY_SHORT // 2
    pad = jnp.pad(proj, ((0, 0), (half, half), (0, 0)))
    u = short_b
    for j in range(HY_SHORT):
        u = u + short_w[j] * pad[:, j:j + n]
    v, g1, g2 = jnp.split(u, 3, axis=-1)
    z = v.astype(jnp.float32)
    for o, gate in enumerate((g1, g2)):
        zs = jnp.fft.rfft(z, n=2 * n, axis=1)
        y = jnp.fft.irfft(zs * spec[None, :, o], n=2 * n, axis=1)[:, :n]
        z = gate.astype(jnp.float32) * (y + skip[o].astype(jnp.float32) * z)
    return z.astype(proj.dtype)


def neighborhood_attention(q, k, v, k_ctx, v_ctx, rpb):
    b, n, nh, hd = q.shape
    rows = n // GRID_W
    kr = min(WIN_R, rows)
    r = jnp.arange(rows)
    col = jnp.arange(GRID_W)
    row_idx = jnp.clip(r - kr // 2, 0, rows - kr)[:, None] + jnp.arange(kr)[None, :]
    c_start = jnp.clip(col - WIN_C // 2, 0, GRID_W - WIN_C)
    col_mask = (col[None, :] >= c_start[:, None]) & (col[None, :] < c_start[:, None] + WIN_C)
    qg = q.reshape(b, rows, GRID_W, nh, hd)
    k_band = k.reshape(b, rows, GRID_W, nh, hd)[:, row_idx]
    v_band = v.reshape(b, rows, GRID_W, nh, hd)[:, row_idx]
    scale = hd ** -0.5
    s_loc = jnp.einsum('brchd,briwhd->bhrciw', qg, k_band).astype(jnp.float32) * scale
    d_row = (row_idx - r[:, None] + WIN_R - 1)[:, None, :, None]
    d_col = jnp.clip(col[None, :] - col[:, None] + WIN_C - 1, 0, 2 * WIN_C - 2)[None, :, None, :]
    bias = rpb.astype(jnp.float32)[:, d_row, d_col]
    s_loc = jnp.where(col_mask[None, None, None, :, None, :], s_loc + bias[None], NEG_INF)
    s_loc = s_loc.reshape(b, nh, rows, GRID_W, kr * GRID_W)
    s_ctx = jnp.einsum('brchd,bkhd->bhrck', qg, k_ctx).astype(jnp.float32) * scale
    p = jax.nn.softmax(jnp.concatenate([s_loc, s_ctx], axis=-1), axis=-1).astype(v.dtype)
    p_loc = p[..., :kr * GRID_W].reshape(b, nh, rows, GRID_W, kr, GRID_W)
    p_ctx = p[..., kr * GRID_W:]
    o = jnp.einsum('bhrciw,briwhd->brchd', p_loc, v_band) + jnp.einsum('bhrck,bkhd->brchd', p_ctx, v_ctx)
    return o.reshape(b, n, nh * hd)


def merge_groups(o_a, o_b, o_c, g):
    o = jnp.concatenate([_rms(o_a), _rms(o_b), _rms(o_c)], axis=-1)
    return (o * g.astype(jnp.float32)).astype(o_a.dtype)


def sq_relu_mlp(u, w1, w2):
    return jnp.square(jax.nn.relu(u @ w1)) @ w2


def setup_inputs(seed: int = 0) -> dict:
    key = jax.random.key(seed)
    ks = jax.random.split(key, 28)
    f32 = jnp.float32
    L = DEPTH
    D = D_MODEL

    def nrm(k, shape, s):
        return s * jax.random.normal(k, shape, f32)

    return {
        'x': nrm(ks[0], (BATCH, SEQ, D), 1.0),
        'c': nrm(ks[1], (BATCH, D), 1.0),
        'ctx': nrm(ks[2], (BATCH, CTX_LEN, D), 1.0),
        'c_ctx': nrm(ks[3], (D,), 1.0),
        'w_mod': nrm(ks[4], (L, D, 6 * D), D ** -0.5),
        'b_mod': nrm(ks[5], (L, 6 * D), 0.02),
        'w_in': nrm(ks[6], (L, D, D_IN), D ** -0.5),
        'q_norm_g': 1.0 + nrm(ks[7], (L, HEAD_DIM), 0.02),
        'k_norm_g': 1.0 + nrm(ks[8], (L, HEAD_DIM), 0.02),
        'hy_short_w': nrm(ks[9], (L, HY_SHORT, 3 * HY_WIDTH), HY_SHORT ** -0.5),
        'hy_short_b': nrm(ks[10], (L, 3 * HY_WIDTH), 0.02),
        'hf_w1': nrm(ks[11], (L, HY_POS_DIM, HY_FFN), HY_POS_DIM ** -0.5),
        'hf_b1': nrm(ks[12], (L, HY_FFN), 0.02),
        'hf_freq': 1.0 + nrm(ks[13], (L, HY_FFN), 0.1),
        'hf_w2': nrm(ks[14], (L, HY_FFN, HY_FFN), HY_FFN ** -0.5),
        'hf_b2': nrm(ks[15], (L, HY_FFN), 0.02),
        'hf_w3': nrm(ks[16], (L, HY_FFN, HY_ORDER * 2 * HY_WIDTH), HY_FFN ** -0.5),
        'hy_bias': nrm(ks[17], (L, HY_ORDER, HY_WIDTH), 1.0),
        'nat_rpb': nrm(ks[18], (L, C_HEADS, 2 * WIN_R - 1, 2 * WIN_C - 1), 0.02),
        'g_mix': 1.0 + nrm(ks[19], (L, D_MIX), 0.02),
        'w_out': nrm(ks[20], (L, D_MIX, D), (D_MIX ** -0.5) * DN_BETA),
        'ln1_g': 1.0 + nrm(ks[21], (L, D), 0.02),
        'ln1_b': nrm(ks[22], (L, D), 0.02),
        'w1': nrm(ks[23], (L, D, D_FF), D ** -0.5),
        'w2': nrm(ks[24], (L, D_FF, D), (D_FF ** -0.5) * DN_BETA),
        'ln2_g': 1.0 + nrm(ks[25], (L, D), 0.02),
        'ln2_b': nrm(ks[26], (L, D), 0.02),
    }


def reference(x, c, ctx, c_ctx, w_mod, b_mod, w_in, q_norm_g, k_norm_g, hy_short_w, hy_short_b,
              hf_w1, hf_b1, hf_freq, hf_w2, hf_b2, hf_w3, hy_bias, nat_rpb, g_mix, w_out,
              ln1_g, ln1_b, w1, w2, ln2_g, ln2_b):
    n_lat = x.shape[1]
    n_ctx = ctx.shape[1]
    cos, sin = axial_rope_tables(n_lat)
    silu_c = jax.nn.silu(c)
    silu_cc = jax.nn.silu(c_ctx)
    h = ctx
    for l in range(DEPTH):
        keep_ctx = l < DEPTH - 1
        mod = silu_c @ w_mod[l] + b_mod[l]
        mod_c = silu_cc @ w_mod[l] + b_mod[l]
        sh1, sc1, gt1, sh2, sc2, gt2 = jnp.split(mod[:, None, :], 6, axis=-1)
        csh1, csc1, cgt1, csh2, csc2, cgt2 = jnp.split(mod_c, 6, axis=-1)

        u = x * (1.0 + sc1) + sh1
        uc = h * (1.0 + csc1) + csh1
        aq, ak, av, hyp, nq, nk, nv = split_projection(u @ w_in[l])
        caq, cak, cav, chyp, cnq, cnk, cnv = split_projection(uc @ w_in[l])
        ak_c = rms_norm(to_heads(cak, A_KV_HEADS), k_norm_g[l])
        av_c = to_heads(cav, A_KV_HEADS)
        nk_c = to_heads(cnk, C_HEADS)
        nv_c = to_heads(cnv, C_HEADS)

        q_a = apply_rope(rms_norm(to_heads(aq, A_HEADS), q_norm_g[l]), cos, sin)
        k_a = apply_rope(rms_norm(to_heads(ak, A_KV_HEADS), k_norm_g[l]), cos, sin)
        o_a = gqa_latent_attention(q_a, k_a, to_heads(av, A_KV_HEADS), ak_c, av_c)
        spec = hyena_filter_spectrum(n_lat, hf_w1[l], hf_b1[l], hf_freq[l], hf_w2[l], hf_b2[l], hf_w3[l])
        o_b = hyena_operator(hyp, hy_short_w[l], hy_short_b[l], spec, hy_bias[l])
        o_c = neighborhood_attention(to_heads(nq, C_HEADS), to_heads(nk, C_HEADS), to_heads(nv, C_HEADS),
                                     nk_c, nv_c, nat_rpb[l])
        mix = merge_groups(o_a, o_b, o_c, g_mix[l]) @ w_out[l]
        x = layer_norm(DN_ALPHA * x + gt1 * mix, ln1_g[l], ln1_b[l])
        if keep_ctx:
            co_a = dense_attention(rms_norm(to_heads(caq, A_HEADS), q_norm_g[l]), ak_c, av_c)
            spec_c = hyena_filter_spectrum(n_ctx, hf_w1[l], hf_b1[l], hf_freq[l], hf_w2[l], hf_b2[l], hf_w3[l])
            co_b = hyena_operator(chyp, hy_short_w[l], hy_short_b[l], spec_c, hy_bias[l])
            co_c = dense_attention(to_heads(cnq, C_HEADS), nk_c, nv_c)
            cmix = merge_groups(co_a, co_b, co_c, g_mix[l]) @ w_out[l]
            h = layer_norm(DN_ALPHA * h + cgt1 * cmix, ln1_g[l], ln1_b[l])

        x = layer_norm(DN_ALPHA * x + gt2 * sq_relu_mlp(x * (1.0 + sc2) + sh2, w1[l], w2[l]), ln2_g[l], ln2_b[l])
        if keep_ctx:
            h = layer_norm(DN_ALPHA * h + cgt2 * sq_relu_mlp(h * (1.0 + csc2) + csh2, w1[l], w2[l]),
                           ln2_g[l], ln2_b[l])
    return x
```

```python
import functools
import math

import jax
import jax.numpy as jnp
from jax import lax
from jax.experimental import pallas as pl
from jax.experimental.pallas import tpu as pltpu

F32 = jnp.float32
BF16 = jnp.bfloat16

HEAD_DIM = 128
GRID_W = 64
A_KV_HEADS = 2
ROPE_THETA = 10000.0
HY_ORDER = 2
HY_DECAY_TARGET = 1e-2
HY_DECAY_SHORT_PCT = 0.3
HY_DECAY_LONG_PCT = 1.5
HY_DECAY_SHIFT = 0.05
EPS = 1e-6
NEG_INF = -1e30
VMEM_LIMIT_BYTES = 56 * 1024 * 1024


def _cparams(*sem):
    return pltpu.CompilerParams(dimension_semantics=sem, vmem_limit_bytes=VMEM_LIMIT_BYTES)


def _tile(n, target, mult=8):
    if n <= target:
        return n
    t = (target // mult) * mult
    while t >= mult:
        if n % t == 0:
            return t
        t -= mult
    return n


def _mod_body(c_ref, w_ref, b_ref, o_ref):
    c = c_ref[...]
    s = (c / (1.0 + jnp.exp(-c))).astype(BF16)
    o_ref[0] = jnp.dot(s, w_ref[0].astype(BF16), preferred_element_type=F32) + b_ref[0]


def _modulation(cc, w_mod, b_mod):
    nl, d, n6 = w_mod.shape
    r = cc.shape[0]
    tn = _tile(n6, 1536, 128)
    return pl.pallas_call(
        _mod_body,
        out_shape=jax.ShapeDtypeStruct((nl, r, n6), F32),
        grid=(nl, n6 // tn),
        in_specs=[pl.BlockSpec((r, d), lambda l, j: (0, 0)),
                  pl.BlockSpec((1, d, tn), lambda l, j: (l, 0, j)),
                  pl.BlockSpec((1, 1, tn), lambda l, j: (l, 0, j))],
        out_specs=pl.BlockSpec((1, r, tn), lambda l, j: (l, 0, j)),
        compiler_params=_cparams("parallel", "parallel"),
        name="modulation",
    )(cc, w_mod, b_mod.reshape(nl, 1, n6))


def _rms_head(xh, g):
    return xh * lax.rsqrt(jnp.mean(xh * xh, axis=-1, keepdims=True) + EPS) * g


def _rope_head(y, cos, sin):
    lane = lax.broadcasted_iota(jnp.int32, y.shape, 1)
    swapped = jnp.where((lane & 1) == 0, pltpu.roll(y, HEAD_DIM - 1, 1), pltpu.roll(y, 1, 1))
    return y * cos + swapped * sin


def _inproj_body(*refs, mode, n_q, n_k, n_heads, rope, scale):
    x_ref, sh_ref, sc_ref, w_ref = refs[:4]
    o_ref = refs[-1]
    u = (x_ref[...] * (1.0 + sc_ref[0, 0]) + sh_ref[0, 0]).astype(BF16)
    acc = jnp.dot(u, w_ref[...], preferred_element_type=F32)
    if mode == "plain":
        o_ref[...] = acc.astype(o_ref.dtype)
        return
    if mode == "nat":
        for hb in range(n_heads):
            xh = acc[:, hb * HEAD_DIM:(hb + 1) * HEAD_DIM]
            if hb < n_q:
                xh = xh * scale
            o_ref[:, hb * HEAD_DIM:(hb + 1) * HEAD_DIM] = xh.astype(o_ref.dtype)
        return
    gq_ref, gk_ref = refs[4:6]
    gq = gq_ref[...] * scale
    gk = gk_ref[...]
    if rope:
        cos = refs[6][...]
        sin = refs[7][...]
    for hb in range(n_heads):
        xh = acc[:, hb * HEAD_DIM:(hb + 1) * HEAD_DIM]
        if hb < n_q + n_k:
            xh = _rms_head(xh, gq if hb < n_q else gk)
            if rope:
                xh = _rope_head(xh, cos, sin)
        o_ref[:, hb * HEAD_DIM:(hb + 1) * HEAD_DIM] = xh.astype(o_ref.dtype)


def _inproj(x, modtab, layer, seq, mod_row_of_tile, w, *, mode, n_q=0, n_k=0,
            gq=None, gk=None, cos=None, sin=None, scale=1.0, tm_target=512):
    r, d = x.shape
    width = w.shape[1]
    tm = _tile(seq, tm_target)
    tiles_per_seq = seq // tm
    rope = cos is not None
    mod_idx = lambda i: mod_row_of_tile(i // tiles_per_seq)
    in_specs = [pl.BlockSpec((tm, d), lambda i: (i, 0)),
                pl.BlockSpec((1, 1, 1, d), lambda i: (layer, mod_idx(i), 0, 0)),
                pl.BlockSpec((1, 1, 1, d), lambda i: (layer, mod_idx(i), 0, 1)),
                pl.BlockSpec((d, width), lambda i: (0, 0))]
    args = [x, modtab, modtab, w]
    if mode == "attn":
        in_specs += [pl.BlockSpec((1, HEAD_DIM), lambda i: (0, 0))] * 2
        args += [gq, gk]
        if rope:
            in_specs += [pl.BlockSpec((tm, HEAD_DIM), lambda i: (i % tiles_per_seq, 0))] * 2
            args += [cos, sin]
    body = functools.partial(_inproj_body, mode=mode, n_q=n_q, n_k=n_k, n_heads=width // HEAD_DIM,
                             rope=rope, scale=scale)
    return pl.pallas_call(
        body,
        out_shape=jax.ShapeDtypeStruct((r, width), BF16),
        grid=(r // tm,),
        in_specs=in_specs,
        out_specs=pl.BlockSpec((tm, width), lambda i: (i, 0)),
        compiler_params=_cparams("parallel"),
        name="inproj_" + mode,
    )(*args)


def _attn_body(*refs, g, n_src):
    q_ref = refs[0]
    k_refs = refs[1:1 + n_src]
    v_refs = refs[1 + n_src:1 + 2 * n_src]
    o_ref = refs[-1]
    tq = q_ref.shape[0]
    q = q_ref[...]
    qs = jnp.concatenate([q[:, h * HEAD_DIM:(h + 1) * HEAD_DIM] for h in range(g)], axis=0) if g > 1 else q
    s = [lax.dot_general(k_ref[...], qs, (((1,), (1,)), ((), ())), preferred_element_type=F32)
         for k_ref in k_refs]
    m = jnp.max(s[0], axis=0, keepdims=True)
    for si in s[1:]:
        m = jnp.maximum(m, jnp.max(si, axis=0, keepdims=True))
    p = [jnp.exp(si - m) for si in s]
    l = jnp.sum(p[0], axis=0, keepdims=True)
    for pi in p[1:]:
        l = l + jnp.sum(pi, axis=0, keepdims=True)
    ot = None
    for v_ref, pi in zip(v_refs, p):
        part = lax.dot_general(v_ref[...], pi.astype(BF16), (((0,), (0,)), ((), ())),
                               preferred_element_type=F32)
        ot = part if ot is None else ot + part
    ot = ot * (1.0 / l)
    for h in range(g):
        o_ref[:, h * HEAD_DIM:(h + 1) * HEAD_DIM] = ot[:, h * tq:(h + 1) * tq].T.astype(o_ref.dtype)


def _attention(qarr, q_col0, n_q_heads, n_kv_heads, srcs, n_seq, seq_q, *, tq_target=256):
    g = n_q_heads // n_kv_heads
    gw = g * HEAD_DIM
    assert q_col0 % gw == 0
    tq = _tile(seq_q, tq_target, 128)
    tiles = seq_q // tq
    q_cb = q_col0 // gw
    in_specs = [pl.BlockSpec((tq, gw), lambda b, h, t: (b * tiles + t, q_cb + h))]
    args = [qarr]
    for which in (1, 2):
        for src in srcs:
            arr, col0, nk = src[0], src[which], src[3]
            cb = col0 // HEAD_DIM
            in_specs.append(pl.BlockSpec((nk, HEAD_DIM), lambda b, h, t, cb=cb: (b, cb + h)))
            args.append(arr)
    return pl.pallas_call(
        functools.partial(_attn_body, g=g, n_src=len(srcs)),
        out_shape=jax.ShapeDtypeStruct((n_seq * seq_q, n_q_heads * HEAD_DIM), F32),
        grid=(n_seq, n_kv_heads, tiles),
        in_specs=in_specs,
        out_specs=pl.BlockSpec((tq, gw), lambda b, h, t: (b * tiles + t, h)),
        compiler_params=_cparams("parallel", "parallel", "parallel"),
        name="attention",
    )(*args)


def _nat_bias_body(rpb_ref, o_ref, *, win_r, win_c, kr):
    h = pl.program_id(0)
    w = GRID_W
    cq = lax.broadcasted_iota(jnp.int32, (w, w), 0)
    ck = lax.broadcasted_iota(jnp.int32, (w, w), 1)
    c_start = jnp.clip(cq - win_c // 2, 0, w - win_c)
    inside = (ck >= c_start) & (ck < c_start + win_c)
    d_col = ck - cq + (win_c - 1)
    n_dr = 2 * win_r - 1
    n_dc = 2 * win_c - 1
    tiles = []
    for dr in range(n_dr):
        t = jnp.zeros((w, w), F32)
        for dc in range(n_dc):
            t = jnp.where(d_col == dc, rpb_ref[(h * n_dr + dr) * n_dc + dc], t)
        tiles.append(jnp.where(inside, t, NEG_INF))
    for v in range(win_r):
        for i in range(kr):
            o_ref[0, v, :, i * w:(i + 1) * w] = tiles[v + i]


def _nat_bias(rpb, kr):
    nh, n_dr, n_dc = rpb.shape
    win_r, win_c = (n_dr + 1) // 2, (n_dc + 1) // 2
    return pl.pallas_call(
        functools.partial(_nat_bias_body, win_r=win_r, win_c=win_c, kr=kr),
        out_shape=jax.ShapeDtypeStruct((nh, win_r, GRID_W, kr * GRID_W), F32),
        grid=(nh,),
        in_specs=[pl.BlockSpec(memory_space=pltpu.SMEM)],
        out_specs=pl.BlockSpec((1, win_r, GRID_W, kr * GRID_W), lambda h: (h, 0, 0, 0)),
        compiler_params=_cparams("parallel"),
        name="nat_bias",
    )(rpb.reshape(-1))


def _nat_body(q_ref, k_ref, v_ref, kc_ref, vc_ref, bias_ref, o_ref, *, rows, kr, win_r):
    w = GRID_W
    kc = kc_ref[...]
    vc = vc_ref[...]

    def row(r, carry):
        start = jnp.clip(r - kr // 2, 0, rows - kr)
        q = q_ref[pl.ds(pl.multiple_of(r * w, w), w), :]
        kb = k_ref[pl.ds(pl.multiple_of(start * w, w), kr * w), :]
        vb = v_ref[pl.ds(pl.multiple_of(start * w, w), kr * w), :]
        s_loc = lax.dot_general(q, kb, (((1,), (1,)), ((), ())), preferred_element_type=F32)
        s_loc = s_loc + bias_ref[0, start - r + win_r - 1]
        s_ctx = lax.dot_general(q, kc, (((1,), (1,)), ((), ())), preferred_element_type=F32)
        m = jnp.maximum(jnp.max(s_loc, axis=-1, keepdims=True), jnp.max(s_ctx, axis=-1, keepdims=True))
        p_loc = jnp.exp(s_loc - m)
        p_ctx = jnp.exp(s_ctx - m)
        l = jnp.sum(p_loc, axis=-1, keepdims=True) + jnp.sum(p_ctx, axis=-1, keepdims=True)
        o = (jnp.dot(p_loc.astype(BF16), vb, preferred_element_type=F32)
             + jnp.dot(p_ctx.astype(BF16), vc, preferred_element_type=F32))
        o_ref[pl.ds(pl.multiple_of(r * w, w), w), :] = (o * (1.0 / l)).astype(o_ref.dtype)
        return carry

    lax.fori_loop(0, rows, row, 0)


def _natten(nat, cnat, bias, n_seq, seq, n_ctx, n_heads):
    rows = seq // GRID_W
    win_r = bias.shape[1]
    kr = bias.shape[3] // GRID_W
    blk = lambda part: pl.BlockSpec((seq, HEAD_DIM), lambda b, h: (b, part * n_heads + h))
    cblk = lambda part: pl.BlockSpec((n_ctx, HEAD_DIM), lambda b, h: (b, part * n_heads + h))
    return pl.pallas_call(
        functools.partial(_nat_body, rows=rows, kr=kr, win_r=win_r),
        out_shape=jax.ShapeDtypeStruct((n_seq * seq, n_heads * HEAD_DIM), F32),
        grid=(n_seq, n_heads),
        in_specs=[blk(0), blk(1), blk(2), cblk(1), cblk(2),
                  pl.BlockSpec((1,) + bias.shape[1:], lambda b, h: (h, 0, 0, 0))],
        out_specs=pl.BlockSpec((seq, HEAD_DIM), lambda b, h: (b, h)),
        compiler_params=_cparams("parallel", "parallel"),
        name="natten",
    )(nat, nat, nat, cnat, cnat, bias)


def _dft_tables(n):
    k = jnp.arange(n, dtype=jnp.int32)
    ks = (k[:, None] * k[None, :]) % (2 * n)
    ang = ks.astype(F32) * (math.pi / n)
    c = jnp.cos(ang)
    s = jnp.sin(ang)
    alt = jnp.where(k % 2 == 0, 1.0, -1.0).astype(F32)
    s = jnp.where(k[:, None] == 0, alt[None, :], s)
    return c.astype(BF16), s.astype(BF16), s.T.astype(BF16)


def _filter_consts(n, n_bands, hy):
    t = jnp.linspace(0.0, 1.0, n, dtype=F32)
    bands = jnp.arange(1, n_bands + 1, dtype=F32)
    ang = 2.0 * math.pi * t[:, None] * bands[None, :]
    feat = jnp.concatenate([t[:, None], jnp.cos(ang), jnp.sin(ang)], axis=-1)
    max_decay = math.log(HY_DECAY_TARGET) / HY_DECAY_SHORT_PCT
    min_decay = math.log(HY_DECAY_TARGET) / HY_DECAY_LONG_PCT
    deltas = jnp.abs(jnp.linspace(min_decay, max_decay, hy, dtype=F32))
    window = jnp.exp(-t[:, None] * deltas[None, :]) + HY_DECAY_SHIFT
    return feat, window


def _filter_body(feat_ref, w1_ref, b1_ref, fr_ref, w2_ref, b2_ref, w3_ref, win_ref, c_ref, s_ref,
                 p_ref, q_ref, r_ref, h_sc, g_sc, hm_sc, nyq_sc, *, n, hy):
    i = pl.program_id(1)
    tm = c_ref.shape[0]
    hp = lax.Precision.HIGHEST
    cw = HEAD_DIM

    @pl.when(i == 0)
    def _():
        fr = fr_ref[0]
        h = jnp.sin(fr * (jnp.dot(feat_ref[...], w1_ref[0], precision=hp, preferred_element_type=F32) + b1_ref[0]))
        h_sc[...] = jnp.sin(fr * (jnp.dot(h, w2_ref[0], precision=hp, preferred_element_type=F32) + b2_ref[0]))
        t_idx = lax.broadcasted_iota(jnp.int32, (n, cw), 0)
        alt = jnp.where((t_idx & 1) == 0, 1.0, -1.0)

        def chunk(j, carry):
            win = win_ref[:, pl.ds(pl.multiple_of(j * cw, cw), cw)]
            for o in range(HY_ORDER):
                col = lambda dr: pl.ds(pl.multiple_of((2 * o + dr) * hy + j * cw, cw), cw)
                out = pl.ds(pl.multiple_of(o * hy + j * cw, cw), cw)
                fwd = jnp.dot(h_sc[...], w3_ref[0, :, col(0)], precision=hp, preferred_element_type=F32) * win
                bwd = jnp.dot(h_sc[...], w3_ref[0, :, col(1)], precision=hp, preferred_element_type=F32) * win
                bwd = jnp.where(t_idx == 0, 0.0, bwd)
                inv = 1.0 / (jnp.sum(jnp.abs(fwd), axis=0, keepdims=True)
                             + jnp.sum(jnp.abs(bwd), axis=0, keepdims=True) + EPS)
                ge = (fwd + bwd) * inv
                g_sc[:, out] = ge.astype(BF16)
                hm_sc[:, out] = ((fwd - bwd) * inv).astype(BF16)
                nyq_sc[:, out] = jnp.broadcast_to(jnp.sum(ge * alt, axis=0, keepdims=True), (8, cw))
            return carry

        lax.fori_loop(0, hy // cw, chunk, 0)

    hr = jnp.dot(c_ref[...], g_sc[...], preferred_element_type=F32)
    hs = jnp.dot(s_ref[...], hm_sc[...], preferred_element_type=F32)
    k_idx = lax.broadcasted_iota(jnp.int32, hr.shape, 0) + i * tm
    dc = k_idx == 0
    wk = jnp.where(dc, 1.0 / (2 * n), 2.0 / (2 * n))
    nyq = nyq_sc[0:1, :]
    pp = hr * wk
    qq = jnp.where(dc, 0.0, hs * wk)
    rr = jnp.where(dc, nyq * wk, pp)
    for o in range(HY_ORDER):
        p_ref[0, o] = pp[:, o * hy:(o + 1) * hy]
        q_ref[0, o] = qq[:, o * hy:(o + 1) * hy]
        r_ref[0, o] = rr[:, o * hy:(o + 1) * hy]


def _filter_spectrum(n, hf_w1, hf_b1, hf_freq, hf_w2, hf_b2, hf_w3, tabs, layers):
    nl = len(layers)
    n_pos, n_ffn = hf_w1.shape[1:]
    hy = hf_w3.shape[2] // (2 * HY_ORDER)
    feat, window = _filter_consts(n, (n_pos - 1) // 2, hy)
    c_tab, s_tab, _ = tabs
    tm = _tile(n, 256, 16)
    lsel = jnp.asarray(layers, jnp.int32)
    lanes = lambda k: -(-k // HEAD_DIM) * HEAD_DIM
    kp, fp = lanes(n_pos), lanes(n_ffn)
    feat = jnp.pad(feat, ((0, 0), (0, kp - n_pos)))
    pick = lambda a, r, c: jnp.pad(a[lsel], ((0, 0), (0, r - a.shape[1]), (0, c - a.shape[2])))
    vec = lambda a: jnp.pad(a[lsel], ((0, 0), (0, fp - a.shape[1]))).reshape(nl, 1, fp)
    n_pos, n_ffn = kp, fp
    full = lambda shp: pl.BlockSpec((1,) + shp, lambda l, i: (l, 0, 0))
    out_sd = jax.ShapeDtypeStruct((nl, HY_ORDER, n, hy), F32)
    out_spec = pl.BlockSpec((1, HY_ORDER, tm, hy), lambda l, i: (l, 0, i, 0))
    return pl.pallas_call(
        functools.partial(_filter_body, n=n, hy=hy),
        out_shape=(out_sd, out_sd, out_sd),
        grid=(nl, n // tm),
        in_specs=[pl.BlockSpec((n, n_pos), lambda l, i: (0, 0)),
                  full((n_pos, n_ffn)), full((1, n_ffn)), full((1, n_ffn)),
                  full((n_ffn, n_ffn)), full((1, n_ffn)), full((n_ffn, 2 * HY_ORDER * hy)),
                  pl.BlockSpec((n, hy), lambda l, i: (0, 0)),
                  pl.BlockSpec((tm, n), lambda l, i: (i, 0)),
                  pl.BlockSpec((tm, n), lambda l, i: (i, 0))],
        out_specs=(out_spec, out_spec, out_spec),
        scratch_shapes=[pltpu.VMEM((n, n_ffn), F32),
                        pltpu.VMEM((n, HY_ORDER * hy), BF16), pltpu.VMEM((n, HY_ORDER * hy), BF16),
                        pltpu.VMEM((8, HY_ORDER * hy), F32)],
        compiler_params=_cparams("parallel", "arbitrary"),
        name="hyena_filter",
    )(feat, pick(hf_w1, kp, fp), vec(hf_b1), vec(hf_freq), pick(hf_w2, fp, fp), vec(hf_b2),
      pick(hf_w3, fp, hf_w3.shape[2]), window, c_tab, s_tab)


def _shortconv_body(x_ref, w_ref, b_ref, o_ref):
    x = x_ref[...].astype(F32)
    n = x.shape[0]
    t = lax.broadcasted_iota(jnp.int32, x.shape, 0)
    prev = jnp.where(t == 0, 0.0, pltpu.roll(x, 1, 0))
    nxt = jnp.where(t == n - 1, 0.0, pltpu.roll(x, n - 1, 0))
    w = w_ref[...]
    o_ref[...] = b_ref[...] + w[0:1] * prev + w[1:2] * x + w[2:3] * nxt


def _shortconv(proj, col0, hy, short_w, short_b, n_seq, seq):
    assert col0 % hy == 0
    cb = col0 // hy
    return pl.pallas_call(
        _shortconv_body,
        out_shape=jax.ShapeDtypeStruct((n_seq * seq, 3 * hy), F32),
        grid=(n_seq, 3),
        in_specs=[pl.BlockSpec((seq, hy), lambda b, j: (b, cb + j)),
                  pl.BlockSpec((short_w.shape[0], hy), lambda b, j: (0, j)),
                  pl.BlockSpec((1, hy), lambda b, j: (0, j))],
        out_specs=pl.BlockSpec((seq, hy), lambda b, j: (b, j)),
        compiler_params=_cparams("parallel", "parallel"),
        name="hyena_shortconv",
    )(proj, short_w, short_b.reshape(1, -1))


def _hy_fwd_body(c_ref, s_ref, z_ref, p_ref, q_ref, r_ref, yr_ref, ys_ref):
    z = z_ref[...].astype(BF16)
    zr = jnp.dot(c_ref[...], z, preferred_element_type=F32)
    zs = jnp.dot(s_ref[...], z, preferred_element_type=F32)
    p, q, r = p_ref[0, 0], q_ref[0, 0], r_ref[0, 0]
    yr_ref[...] = (zr * p - zs * q).astype(yr_ref.dtype)
    ys_ref[...] = (zr * q + zs * r).astype(ys_ref.dtype)


def _hy_inv_body(c_ref, st_ref, yr_ref, ys_ref, gate_ref, z_ref, skip_ref, o_ref):
    y = (jnp.dot(c_ref[...], yr_ref[...], preferred_element_type=F32)
         + jnp.dot(st_ref[...], ys_ref[...], preferred_element_type=F32))
    o_ref[...] = gate_ref[...] * (y + skip_ref[0] * z_ref[...])


def _hyena_order(z_arr, z_cb, gate_arr, gate_cb, spec, li, order, skip, tabs, n_seq, seq, hy):
    c_tab, s_tab, st_tab = tabs
    p_arr, q_arr, r_arr = spec
    tm = _tile(seq, 512, 16)
    tiles = seq // tm
    tab_spec = pl.BlockSpec((tm, seq), lambda b, i: (i, 0))
    spec_spec = pl.BlockSpec((1, 1, tm, hy), lambda b, i: (li, order, i, 0))
    y_sd = jax.ShapeDtypeStruct((n_seq * seq, hy), BF16)
    yr, ys = pl.pallas_call(
        _hy_fwd_body,
        out_shape=(y_sd, y_sd),
        grid=(n_seq, tiles),
        in_specs=[tab_spec, tab_spec,
                  pl.BlockSpec((seq, hy), lambda b, i: (b, z_cb)),
                  spec_spec, spec_spec, spec_spec],
        out_specs=(pl.BlockSpec((tm, hy), lambda b, i: (b * tiles + i, 0)),) * 2,
        compiler_params=_cparams("parallel", "parallel"),
        name="hyena_fwd_dft",
    )(c_tab, s_tab, z_arr, p_arr, q_arr, r_arr)
    return pl.pallas_call(
        _hy_inv_body,
        out_shape=jax.ShapeDtypeStruct((n_seq * seq, hy), F32),
        grid=(n_seq, tiles),
        in_specs=[tab_spec, tab_spec,
                  pl.BlockSpec((seq, hy), lambda b, i: (b, 0)),
                  pl.BlockSpec((seq, hy), lambda b, i: (b, 0)),
                  pl.BlockSpec((tm, hy), lambda b, i: (b * tiles + i, gate_cb)),
                  pl.BlockSpec((tm, hy), lambda b, i: (b * tiles + i, z_cb)),
                  pl.BlockSpec((1, 1, hy), lambda b, i: (order, 0, 0))],
        out_specs=pl.BlockSpec((tm, hy), lambda b, i: (b * tiles + i, 0)),
        compiler_params=_cparams("parallel", "parallel"),
        name="hyena_inv_dft",
    )(c_tab, st_tab, yr, ys, gate_arr, z_arr, skip.reshape(HY_ORDER, 1, hy))


def _hyena(proj, col0, hy, short_w, short_b, spec, li, skip, tabs, n_seq, seq):
    u = _shortconv(proj, col0, hy, short_w, short_b, n_seq, seq)
    z = _hyena_order(u, 0, u, 1, spec, li, 0, skip, tabs, n_seq, seq, hy)
    return _hyena_order(z, 0, u, 2, spec, li, 1, skip, tabs, n_seq, seq, hy)


def _layer_norm(y, g, b):
    mu = jnp.mean(y, axis=-1, keepdims=True)
    yc = y - mu
    var = jnp.mean(yc * yc, axis=-1, keepdims=True)
    return yc * lax.rsqrt(var + EPS) * g + b


def _merge_body(oa_ref, ob_ref, oc_ref, x_ref, gt_ref, g_ref, w_ref, lg_ref, lb_ref, o_ref, mix_sc, *, alpha):
    col = 0
    for ref in (oa_ref, ob_ref, oc_ref):
        v = ref[...]
        wd = v.shape[1]
        vn = v * lax.rsqrt(jnp.mean(v * v, axis=-1, keepdims=True) + EPS)
        mix_sc[:, col:col + wd] = (vn * g_ref[:, col:col + wd]).astype(BF16)
        col += wd
    mix = jnp.dot(mix_sc[...], w_ref[...], preferred_element_type=F32)
    y = alpha * x_ref[...] + gt_ref[0, 0] * mix
    o_ref[...] = _layer_norm(y, lg_ref[...], lb_ref[...])


def _merge(oa, ob, oc, x, modtab, layer, seq, mod_row_of_tile, g_mix, w_out, ln_g, ln_b, alpha, *, tm_target=256):
    r, d = x.shape
    tm = _tile(seq, tm_target)
    tiles_per_seq = seq // tm
    mod_idx = lambda i: mod_row_of_tile(i // tiles_per_seq)
    row = lambda a: pl.BlockSpec((tm, a.shape[1]), lambda i: (i, 0))
    vec = pl.BlockSpec((1, d), lambda i: (0, 0))
    return pl.pallas_call(
        functools.partial(_merge_body, alpha=alpha),
        out_shape=jax.ShapeDtypeStruct((r, d), F32),
        grid=(r // tm,),
        in_specs=[row(oa), row(ob), row(oc), row(x),
                  pl.BlockSpec((1, 1, 1, d), lambda i: (layer, mod_idx(i), 0, 2)),
                  vec, pl.BlockSpec(w_out.shape, lambda i: (0, 0)), vec, vec],
        out_specs=pl.BlockSpec((tm, d), lambda i: (i, 0)),
        scratch_shapes=[pltpu.VMEM((tm, w_out.shape[0]), BF16)],
        compiler_params=_cparams("parallel"),
        name="merge_outproj_ln",
    )(oa, ob, oc, x, modtab, g_mix.reshape(1, -1), w_out, ln_g.reshape(1, -1), ln_b.reshape(1, -1))


def _mlp_body(x_ref, sh_ref, sc_ref, gt_ref, w1_ref, w2_ref, lg_ref, lb_ref, o_ref, u_sc, acc_sc, *, alpha):
    f = pl.program_id(1)

    @pl.when(f == 0)
    def _():
        u_sc[...] = (x_ref[...] * (1.0 + sc_ref[0, 0]) + sh_ref[0, 0]).astype(BF16)
        acc_sc[...] = jnp.zeros_like(acc_sc)

    h = jnp.maximum(jnp.dot(u_sc[...], w1_ref[...], preferred_element_type=F32), 0.0)
    acc_sc[...] += jnp.dot((h * h).astype(BF16), w2_ref[...], preferred_element_type=F32)

    @pl.when(f == pl.num_programs(1) - 1)
    def _():
        y = alpha * x_ref[...] + gt_ref[0, 0] * acc_sc[...]
        o_ref[...] = _layer_norm(y, lg_ref[...], lb_ref[...])


def _mlp(x, modtab, layer, seq, mod_row_of_tile, w1, w2, ln_g, ln_b, alpha, *, tm_target=512, tf_target=512):
    r, d = x.shape
    dff = w1.shape[1]
    tm = _tile(seq, tm_target)
    tf = _tile(dff, tf_target, 128)
    tiles_per_seq = seq // tm
    mod_idx = lambda i: mod_row_of_tile(i // tiles_per_seq)
    mod = lambda k: pl.BlockSpec((1, 1, 1, d), lambda i, f: (layer, mod_idx(i), 0, k))
    vec = pl.BlockSpec((1, d), lambda i, f: (0, 0))
    return pl.pallas_call(
        functools.partial(_mlp_body, alpha=alpha),
        out_shape=jax.ShapeDtypeStruct((r, d), F32),
        grid=(r // tm, dff // tf),
        in_specs=[pl.BlockSpec((tm, d), lambda i, f: (i, 0)), mod(3), mod(4), mod(5),
                  pl.BlockSpec((d, tf), lambda i, f: (0, f)),
                  pl.BlockSpec((tf, d), lambda i, f: (f, 0)), vec, vec],
        out_specs=pl.BlockSpec((tm, d), lambda i, f: (i, 0)),
        scratch_shapes=[pltpu.VMEM((tm, d), BF16), pltpu.VMEM((tm, d), F32)],
        compiler_params=_cparams("parallel", "arbitrary"),
        name="mlp_ln",
    )(x, modtab, modtab, modtab, w1, w2, ln_g.reshape(1, -1), ln_b.reshape(1, -1))


def _rope_tables(n):
    t = jnp.arange(n)
    row = (t // GRID_W).astype(F32)
    col = (t % GRID_W).astype(F32)
    n_pairs_axis = HEAD_DIM // 4
    inv_freq = ROPE_THETA ** (-jnp.arange(n_pairs_axis, dtype=F32) / n_pairs_axis)
    ang = jnp.concatenate([row[:, None] * inv_freq[None, :], col[:, None] * inv_freq[None, :]], axis=-1)
    cos = jnp.repeat(jnp.cos(ang), 2, axis=-1)
    sin = jnp.stack([-jnp.sin(ang), jnp.sin(ang)], axis=-1).reshape(n, HEAD_DIM)
    return cos, sin


def kernel(x, c, ctx, c_ctx, w_mod, b_mod, w_in, q_norm_g, k_norm_g, hy_short_w, hy_short_b, hf_w1, hf_b1,
           hf_freq, hf_w2, hf_b2, hf_w3, hy_bias, nat_rpb, g_mix, w_out, ln1_g, ln1_b, w1, w2, ln2_g, ln2_b):
    bsz, n_lat, d = x.shape
    n_ctx = ctx.shape[1]
    depth = w_mod.shape[0]
    n_mix_heads = d // HEAD_DIM
    a_heads = n_mix_heads // 2
    hy = (n_mix_heads // 4) * HEAD_DIM
    c_heads = n_mix_heads - a_heads - hy // HEAD_DIM
    a_w, kv_w, c_w = a_heads * HEAD_DIM, A_KV_HEADS * HEAD_DIM, c_heads * HEAD_DIM
    attn_w = a_w + 2 * kv_w
    hy_col0 = attn_w
    nat_col0 = attn_w + 3 * hy
    assert w_in.shape[2] == nat_col0 + 3 * c_w and n_lat % GRID_W == 0
    alpha = (2 * depth) ** 0.25
    scale = HEAD_DIM ** -0.5
    rows = n_lat // GRID_W
    kr = min((nat_rpb.shape[2] + 1) // 2, rows)

    n_rows = -(-(bsz + 1) // 8) * 8
    cc = jnp.zeros((n_rows, d), F32).at[:bsz].set(c).at[bsz].set(c_ctx)
    modtab = _modulation(cc, w_mod, b_mod).reshape(depth, n_rows, 1, 6 * d)
    lat_mod = lambda b: b
    ctx_mod = lambda b: bsz

    cos, sin = _rope_tables(n_lat)
    tabs_lat = _dft_tables(n_lat)
    spec_lat = _filter_spectrum(n_lat, hf_w1, hf_b1, hf_freq, hf_w2, hf_b2, hf_w3, tabs_lat, list(range(depth)))
    if depth > 1:
        tabs_ctx = _dft_tables(n_ctx)
        spec_ctx = _filter_spectrum(n_ctx, hf_w1, hf_b1, hf_freq, hf_w2, hf_b2, hf_w3, tabs_ctx,
                                    list(range(depth - 1)))

    xl = x.reshape(bsz * n_lat, d)
    hc = ctx.reshape(bsz * n_ctx, d)
    for l in range(depth):
        keep_ctx = l < depth - 1
        w_attn = w_in[l, :, :attn_w].astype(BF16)
        w_hyp = w_in[l, :, hy_col0:nat_col0].astype(BF16)
        w_nat = w_in[l, :, nat_col0:].astype(BF16)
        gq = q_norm_g[l].reshape(1, HEAD_DIM)
        gk = k_norm_g[l].reshape(1, HEAD_DIM)
        bias = _nat_bias(nat_rpb[l], kr)

        attn_kw = dict(mode="attn", n_q=a_heads, n_k=A_KV_HEADS, gq=gq, gk=gk, scale=scale)
        nat_kw = dict(mode="nat", n_q=c_heads, scale=scale)
        qkv = _inproj(xl, modtab, l, n_lat, lat_mod, w_attn, cos=cos, sin=sin, **attn_kw)
        cqkv = _inproj(hc, modtab, l, n_ctx, ctx_mod, w_attn, **attn_kw)
        hyp = _inproj(xl, modtab, l, n_lat, lat_mod, w_hyp, mode="plain")
        nat = _inproj(xl, modtab, l, n_lat, lat_mod, w_nat, **nat_kw)
        cnat = _inproj(hc, modtab, l, n_ctx, ctx_mod, w_nat, **nat_kw)

        o_a = _attention(qkv, 0, a_heads, A_KV_HEADS,
                         [(qkv, a_w, a_w + kv_w, n_lat), (cqkv, a_w, a_w + kv_w, n_ctx)], bsz, n_lat)
        o_b = _hyena(hyp, 0, hy, hy_short_w[l], hy_short_b[l], spec_lat, l, hy_bias[l], tabs_lat, bsz, n_lat)
        o_c = _natten(nat, cnat, bias, bsz, n_lat, n_ctx, c_heads)
        w_out_l = w_out[l].astype(BF16)
        xl_new = _merge(o_a, o_b, o_c, xl, modtab, l, n_lat, lat_mod, g_mix[l], w_out_l, ln1_g[l], ln1_b[l], alpha)

        w1_l = w1[l].astype(BF16)
        w2_l = w2[l].astype(BF16)
        if keep_ctx:
            chyp = _inproj(hc, modtab, l, n_ctx, ctx_mod, w_hyp, mode="plain")
            co_a = _attention(cqkv, 0, a_heads, A_KV_HEADS, [(cqkv, a_w, a_w + kv_w, n_ctx)], bsz, n_ctx)
            co_b = _hyena(chyp, 0, hy, hy_short_w[l], hy_short_b[l], spec_ctx, l, hy_bias[l], tabs_ctx, bsz, n_ctx)
            co_c = _attention(cnat, 0, c_heads, c_heads, [(cnat, c_w, 2 * c_w, n_ctx)], bsz, n_ctx)
            hc = _merge(co_a, co_b, co_c, hc, modtab, l, n_ctx, ctx_mod, g_mix[l], w_out_l, ln1_g[l], ln1_b[l], alpha)
            hc = _mlp(hc, modtab, l, bsz * n_ctx, ctx_mod, w1_l, w2_l, ln2_g[l], ln2_b[l], alpha)
        xl = _mlp(xl_new, modtab, l, n_lat, lat_mod, w1_l, w2_l, ln2_g[l], ln2_b[l], alpha)
    return xl.reshape(bsz, n_lat, d)
```

```python
import functools
import math

import jax
import jax.numpy as jnp
import numpy as np
from jax import lax
from jax.experimental import pallas as pl
from jax.experimental.pallas import tpu as pltpu

F32 = jnp.float32
BF16 = jnp.bfloat16

HEAD_DIM = 128
GRID_W = 64
A_KV_HEADS = 2
ROPE_THETA = 10000.0
HY_ORDER = 2
HY_DECAY_TARGET = 1e-2
HY_DECAY_SHORT_PCT = 0.3
HY_DECAY_LONG_PCT = 1.5
HY_DECAY_SHIFT = 0.05
EPS = 1e-6
NEG_INF = -1e30
LOG2E = math.log2(math.e)
VMEM_LIMIT_BYTES = 56 * 1024 * 1024


def _cparams(*sem):
    return pltpu.CompilerParams(dimension_semantics=sem, vmem_limit_bytes=VMEM_LIMIT_BYTES)


def _tile(n, target, mult=8):
    if n <= target:
        return n
    t = (target // mult) * mult
    while t >= mult:
        if n % t == 0:
            return t
        t -= mult
    return n


def _mod_body(c_ref, w_ref, b_ref, o_ref):
    c = c_ref[...]
    s = (c / (1.0 + jnp.exp(-c))).astype(BF16)
    o_ref[0] = jnp.dot(s, w_ref[0].astype(BF16), preferred_element_type=F32) + b_ref[0]


def _modulation(cc, w_mod, b_mod):
    nl, d, n6 = w_mod.shape
    r = cc.shape[0]
    tn = _tile(n6, 1536, 128)
    return pl.pallas_call(
        _mod_body,
        out_shape=jax.ShapeDtypeStruct((nl, r, n6), F32),
        grid=(nl, n6 // tn),
        in_specs=[pl.BlockSpec((r, d), lambda l, j: (0, 0)),
                  pl.BlockSpec((1, d, tn), lambda l, j: (l, 0, j)),
                  pl.BlockSpec((1, 1, tn), lambda l, j: (l, 0, j))],
        out_specs=pl.BlockSpec((1, r, tn), lambda l, j: (l, 0, j)),
        compiler_params=_cparams("parallel", "parallel"),
        name="modulation",
    )(cc, w_mod, b_mod.reshape(nl, 1, n6))


def _rms_head(xh, g):
    return xh * lax.rsqrt(jnp.mean(xh * xh, axis=-1, keepdims=True) + EPS) * g


def _rope_head(y, cos, sin):
    lane = lax.broadcasted_iota(jnp.int32, y.shape, 1)
    swapped = jnp.where((lane & 1) == 0, pltpu.roll(y, HEAD_DIM - 1, 1), pltpu.roll(y, 1, 1))
    return y * cos + swapped * sin


def _inproj_body(*refs, mode, n_q, n_k, n_heads, rope, scale):
    x_ref, sh_ref, sc_ref, w_ref = refs[:4]
    o_ref = refs[-1]
    u = (x_ref[...] * (1.0 + sc_ref[0, 0]) + sh_ref[0, 0]).astype(BF16)
    acc = jnp.dot(u, w_ref[...], preferred_element_type=F32)
    if mode == "plain":
        o_ref[...] = acc.astype(o_ref.dtype)
        return
    if mode == "nat":
        for hb in range(n_heads):
            xh = acc[:, hb * HEAD_DIM:(hb + 1) * HEAD_DIM]
            if hb < n_q:
                xh = xh * scale
            o_ref[:, hb * HEAD_DIM:(hb + 1) * HEAD_DIM] = xh.astype(o_ref.dtype)
        return
    gq_ref, gk_ref = refs[4:6]
    gq = gq_ref[...] * scale
    gk = gk_ref[...]
    if rope:
        cos = refs[6][...]
        sin = refs[7][...]
    for hb in range(n_heads):
        xh = acc[:, hb * HEAD_DIM:(hb + 1) * HEAD_DIM]
        if hb < n_q + n_k:
            xh = _rms_head(xh, gq if hb < n_q else gk)
            if rope:
                xh = _rope_head(xh, cos, sin)
        o_ref[:, hb * HEAD_DIM:(hb + 1) * HEAD_DIM] = xh.astype(o_ref.dtype)


def _inproj(x, modtab, layer, seq, mod_row_of_tile, w, *, mode, n_q=0, n_k=0,
            gq=None, gk=None, cos=None, sin=None, scale=1.0, tm_target=512):
    r, d = x.shape
    width = w.shape[1]
    tm = _tile(seq, tm_target)
    tiles_per_seq = seq // tm
    rope = cos is not None
    mod_idx = lambda i: mod_row_of_tile(i // tiles_per_seq)
    in_specs = [pl.BlockSpec((tm, d), lambda i: (i, 0)),
                pl.BlockSpec((1, 1, 1, d), lambda i: (layer, mod_idx(i), 0, 0)),
                pl.BlockSpec((1, 1, 1, d), lambda i: (layer, mod_idx(i), 0, 1)),
                pl.BlockSpec((d, width), lambda i: (0, 0))]
    args = [x, modtab, modtab, w]
    if mode == "attn":
        in_specs += [pl.BlockSpec((1, HEAD_DIM), lambda i: (0, 0))] * 2
        args += [gq, gk]
        if rope:
            in_specs += [pl.BlockSpec((tm, HEAD_DIM), lambda i: (i % tiles_per_seq, 0))] * 2
            args += [cos, sin]
    body = functools.partial(_inproj_body, mode=mode, n_q=n_q, n_k=n_k, n_heads=width // HEAD_DIM,
                             rope=rope, scale=scale)
    return pl.pallas_call(
        body,
        out_shape=jax.ShapeDtypeStruct((r, width), BF16),
        grid=(r // tm,),
        in_specs=in_specs,
        out_specs=pl.BlockSpec((tm, width), lambda i: (i, 0)),
        compiler_params=_cparams("parallel"),
        name="inproj_" + mode,
    )(*args)


def _softmax_pv_t(s, vs):
    m = jnp.max(s[0], axis=0, keepdims=True)
    for si in s[1:]:
        m = jnp.maximum(m, jnp.max(si, axis=0, keepdims=True))
    p = [jnp.exp2(si - m) for si in s]
    l = jnp.sum(p[0], axis=0, keepdims=True)
    for pi in p[1:]:
        l = l + jnp.sum(pi, axis=0, keepdims=True)
    ot = None
    for v, pi in zip(vs, p):
        part = lax.dot_general(v, pi.astype(BF16), (((0,), (0,)), ((), ())), preferred_element_type=F32)
        ot = part if ot is None else ot + part
    return ot * (1.0 / l)


def _attn_body(*refs, g, n_src, n_sub):
    q_ref = refs[0]
    k_refs = refs[1:1 + n_src]
    v_refs = refs[1 + n_src:1 + 2 * n_src]
    o_ref = refs[-1]
    tq = q_ref.shape[0]
    hs = g // n_sub
    scores = []
    for j in range(n_sub):
        heads = range(j * hs, (j + 1) * hs)
        qs = jnp.concatenate([q_ref[:, h * HEAD_DIM:(h + 1) * HEAD_DIM] for h in heads], axis=0)
        scores.append([lax.dot_general(k_ref[...], qs, (((1,), (1,)), ((), ())), preferred_element_type=F32)
                       for k_ref in k_refs])
    for j in range(n_sub):
        ot = _softmax_pv_t(scores[j], [v_ref[...] for v_ref in v_refs])
        for i in range(hs):
            h = j * hs + i
            o_ref[:, h * HEAD_DIM:(h + 1) * HEAD_DIM] = ot[:, i * tq:(i + 1) * tq].T.astype(o_ref.dtype)


def _attention(qarr, q_col0, n_q_heads, n_kv_heads, srcs, n_seq, seq_q, *, tq_target=256):
    g = n_q_heads // n_kv_heads
    gw = g * HEAD_DIM
    assert q_col0 % gw == 0
    tq = _tile(seq_q, tq_target, 128)
    tiles = seq_q // tq
    q_cb = q_col0 // gw
    in_specs = [pl.BlockSpec((tq, gw), lambda b, h, t: (b * tiles + t, q_cb + h))]
    args = [qarr]
    for which in (1, 2):
        for src in srcs:
            arr, col0, nk = src[0], src[which], src[3]
            cb = col0 // HEAD_DIM
            in_specs.append(pl.BlockSpec((nk, HEAD_DIM), lambda b, h, t, cb=cb: (b, cb + h)))
            args.append(arr)
    return pl.pallas_call(
        functools.partial(_attn_body, g=g, n_src=len(srcs), n_sub=g),
        out_shape=jax.ShapeDtypeStruct((n_seq * seq_q, n_q_heads * HEAD_DIM), F32),
        grid=(n_seq, n_kv_heads, tiles),
        in_specs=in_specs,
        out_specs=pl.BlockSpec((tq, gw), lambda b, h, t: (b * tiles + t, h)),
        compiler_params=_cparams("parallel", "parallel", "parallel"),
        name="attention",
    )(*args)


NAT_GROUP_ROWS = 4


def _nat_plan(rows, kr):
    ub_rows = min(kr + NAT_GROUP_ROWS, rows)
    assert rows % NAT_GROUP_ROWS == 0
    starts, pids, patterns = [], [], []
    for r0 in range(0, rows, NAT_GROUP_ROWS):
        band = lambda r: min(max(r - kr // 2, 0), rows - kr)
        ub = min(band(r0), rows - ub_rows)
        pat = tuple((band(r) - ub, r - ub) for r in range(r0, r0 + NAT_GROUP_ROWS))
        if pat not in patterns:
            patterns.append(pat)
        starts.append(ub)
        pids.append(patterns.index(pat))
    return ub_rows, starts, pids, patterns


def _nat_bias_body(rpb_ref, o_ref, *, win_r, win_c, kr, ub_rows, patterns):
    h = pl.program_id(0)
    w = GRID_W
    ck = lax.broadcasted_iota(jnp.int32, (w, w), 0)
    cq = lax.broadcasted_iota(jnp.int32, (w, w), 1)
    c_start = jnp.clip(cq - win_c // 2, 0, w - win_c)
    inside = (ck >= c_start) & (ck < c_start + win_c)
    d_col = ck - cq + (win_c - 1)
    n_dr = 2 * win_r - 1
    n_dc = 2 * win_c - 1
    tiles = []
    for dr in range(n_dr):
        t = jnp.zeros((w, w), F32)
        for dc in range(n_dc):
            t = jnp.where(d_col == dc, rpb_ref[(h * n_dr + dr) * n_dc + dc], t)
        tiles.append(jnp.where(inside, t * LOG2E, NEG_INF))
    masked = jnp.full((w, w), NEG_INF, F32)
    for p, pat in enumerate(patterns):
        for rq, (s_rel, r_rel) in enumerate(pat):
            for kk in range(ub_rows):
                blk = tiles[kk - r_rel + win_r - 1] if s_rel <= kk < s_rel + kr else masked
                o_ref[0, p, kk * w:(kk + 1) * w, rq * w:(rq + 1) * w] = blk


def _nat_bias(rpb, kr, ub_rows, patterns):
    nh, n_dr, n_dc = rpb.shape
    win_r, win_c = (n_dr + 1) // 2, (n_dc + 1) // 2
    shape = (len(patterns), ub_rows * GRID_W, NAT_GROUP_ROWS * GRID_W)
    return pl.pallas_call(
        functools.partial(_nat_bias_body, win_r=win_r, win_c=win_c, kr=kr, ub_rows=ub_rows, patterns=patterns),
        out_shape=jax.ShapeDtypeStruct((nh,) + shape, F32),
        grid=(nh,),
        in_specs=[pl.BlockSpec(memory_space=pltpu.SMEM)],
        out_specs=pl.BlockSpec((1,) + shape, lambda h: (h, 0, 0, 0)),
        compiler_params=_cparams("parallel"),
        name="nat_bias",
    )(rpb.reshape(-1))


def _nat_body(q_ref, k_ref, v_ref, kc_ref, vc_ref, bias_ref, o_ref, *, starts, pids):
    w = GRID_W
    gq = NAT_GROUP_ROWS * w
    nk = bias_ref.shape[2]
    nt = (((1,), (1,)), ((), ()))
    scores = []
    for g, (ub, pid) in enumerate(zip(starts, pids)):
        q = q_ref[g * gq:(g + 1) * gq, :]
        s_loc = lax.dot_general(k_ref[ub * w:ub * w + nk, :], q, nt, preferred_element_type=F32)
        s_ctx = lax.dot_general(kc_ref[...], q, nt, preferred_element_type=F32)
        scores.append([s_loc + bias_ref[0, pid], s_ctx])
    for g, ub in enumerate(starts):
        ot = _softmax_pv_t(scores[g], [v_ref[ub * w:ub * w + nk, :], vc_ref[...]])
        o_ref[g * gq:(g + 1) * gq, :] = ot.T.astype(o_ref.dtype)


def _natten(nat, cnat, bias, starts, pids, n_seq, seq, n_ctx, n_heads):
    blk = lambda part: pl.BlockSpec((seq, HEAD_DIM), lambda h, b: (b, part * n_heads + h))
    cblk = lambda part: pl.BlockSpec((n_ctx, HEAD_DIM), lambda h, b: (b, part * n_heads + h))
    return pl.pallas_call(
        functools.partial(_nat_body, starts=tuple(starts), pids=tuple(pids)),
        out_shape=jax.ShapeDtypeStruct((n_seq * seq, n_heads * HEAD_DIM), F32),
        grid=(n_heads, n_seq),
        in_specs=[blk(0), blk(1), blk(2), cblk(1), cblk(2),
                  pl.BlockSpec((1,) + bias.shape[1:], lambda h, b: (h, 0, 0, 0))],
        out_specs=pl.BlockSpec((seq, HEAD_DIM), lambda h, b: (b, h)),
        compiler_params=_cparams("parallel", "parallel"),
        name="natten",
    )(nat, nat, nat, cnat, cnat, bias)


@functools.lru_cache(maxsize=None)
def _dft_tables(n):
    k = np.arange(n, dtype=np.int64)
    ang = ((k[:, None] * k[None, :]) % (2 * n)).astype(np.float64) * (math.pi / n)
    c = np.cos(ang)
    s = np.sin(ang)
    s[0, :] = np.where(k % 2 == 0, 1.0, -1.0)
    to_bf16 = lambda a: np.asarray(a, np.float32).astype(BF16)
    return to_bf16(c), to_bf16(s), to_bf16(s.T)


@functools.lru_cache(maxsize=None)
def _filter_consts(n, n_bands, hy):
    t = np.linspace(0.0, 1.0, n)
    bands = np.arange(1, n_bands + 1, dtype=np.float64)
    ang = 2.0 * math.pi * t[:, None] * bands[None, :]
    feat = np.concatenate([t[:, None], np.cos(ang), np.sin(ang)], axis=-1)
    max_decay = math.log(HY_DECAY_TARGET) / HY_DECAY_SHORT_PCT
    min_decay = math.log(HY_DECAY_TARGET) / HY_DECAY_LONG_PCT
    deltas = np.abs(np.linspace(min_decay, max_decay, hy))
    window = np.exp(-t[:, None] * deltas[None, :]) + HY_DECAY_SHIFT
    return feat.astype(np.float32), window.astype(np.float32)


def _filter_body(feat_ref, w1_ref, b1_ref, fr_ref, w2_ref, b2_ref, w3_ref, win_ref, c_ref, s_ref,
                 p_ref, q_ref, r_ref, h_sc, g_sc, hm_sc, nyq_sc, *, n, hy):
    i = pl.program_id(1)
    tm = c_ref.shape[0]
    hp = lax.Precision.HIGHEST
    cw = HEAD_DIM

    @pl.when(i == 0)
    def _():
        fr = fr_ref[0]
        h = jnp.sin(fr * (jnp.dot(feat_ref[...], w1_ref[0], precision=hp, preferred_element_type=F32) + b1_ref[0]))
        h_sc[...] = jnp.sin(fr * (jnp.dot(h, w2_ref[0], precision=hp, preferred_element_type=F32) + b2_ref[0]))
        t_idx = lax.broadcasted_iota(jnp.int32, (n, cw), 0)
        alt = jnp.where((t_idx & 1) == 0, 1.0, -1.0)

        def chunk(j, carry):
            win = win_ref[:, pl.ds(pl.multiple_of(j * cw, cw), cw)]
            for o in range(HY_ORDER):
                col = lambda dr: pl.ds(pl.multiple_of((2 * o + dr) * hy + j * cw, cw), cw)
                out = pl.ds(pl.multiple_of(o * hy + j * cw, cw), cw)
                fwd = jnp.dot(h_sc[...], w3_ref[0, :, col(0)], precision=hp, preferred_element_type=F32) * win
                bwd = jnp.dot(h_sc[...], w3_ref[0, :, col(1)], precision=hp, preferred_element_type=F32) * win
                bwd = jnp.where(t_idx == 0, 0.0, bwd)
                inv = 1.0 / (jnp.sum(jnp.abs(fwd), axis=0, keepdims=True)
                             + jnp.sum(jnp.abs(bwd), axis=0, keepdims=True) + EPS)
                ge = (fwd + bwd) * inv
                g_sc[:, out] = ge.astype(BF16)
                hm_sc[:, out] = ((fwd - bwd) * inv).astype(BF16)
                nyq_sc[:, out] = jnp.broadcast_to(jnp.sum(ge * alt, axis=0, keepdims=True), (8, cw))
            return carry

        lax.fori_loop(0, hy // cw, chunk, 0)

    hr = jnp.dot(c_ref[...], g_sc[...], preferred_element_type=F32)
    hs = jnp.dot(s_ref[...], hm_sc[...], preferred_element_type=F32)
    k_idx = lax.broadcasted_iota(jnp.int32, hr.shape, 0) + i * tm
    dc = k_idx == 0
    wk = jnp.where(dc, 1.0 / (2 * n), 2.0 / (2 * n))
    nyq = nyq_sc[0:1, :]
    pp = hr * wk
    qq = jnp.where(dc, 0.0, hs * wk)
    rr = jnp.where(dc, nyq * wk, pp)
    for o in range(HY_ORDER):
        p_ref[0, o] = pp[:, o * hy:(o + 1) * hy]
        q_ref[0, o] = qq[:, o * hy:(o + 1) * hy]
        r_ref[0, o] = rr[:, o * hy:(o + 1) * hy]


def _filter_spectrum(n, hf_w1, hf_b1, hf_freq, hf_w2, hf_b2, hf_w3, tabs, layers):
    nl = len(layers)
    n_pos, n_ffn = hf_w1.shape[1:]
    hy = hf_w3.shape[2] // (2 * HY_ORDER)
    feat, window = _filter_consts(n, (n_pos - 1) // 2, hy)
    c_tab, s_tab, _ = tabs
    tm = _tile(n, 256, 16)
    lsel = jnp.asarray(layers, jnp.int32)
    lanes = lambda k: -(-k // HEAD_DIM) * HEAD_DIM
    kp, fp = lanes(n_pos), lanes(n_ffn)
    feat = np.pad(feat, ((0, 0), (0, kp - n_pos)))
    pick = lambda a, r, c: jnp.pad(a[lsel], ((0, 0), (0, r - a.shape[1]), (0, c - a.shape[2])))
    vec = lambda a: jnp.pad(a[lsel], ((0, 0), (0, fp - a.shape[1]))).reshape(nl, 1, fp)
    n_pos, n_ffn = kp, fp
    full = lambda shp: pl.BlockSpec((1,) + shp, lambda l, i: (l, 0, 0))
    out_sd = jax.ShapeDtypeStruct((nl, HY_ORDER, n, hy), F32)
    out_spec = pl.BlockSpec((1, HY_ORDER, tm, hy), lambda l, i: (l, 0, i, 0))
    return pl.pallas_call(
        functools.partial(_filter_body, n=n, hy=hy),
        out_shape=(out_sd, out_sd, out_sd),
        grid=(nl, n // tm),
        in_specs=[pl.BlockSpec((n, n_pos), lambda l, i: (0, 0)),
                  full((n_pos, n_ffn)), full((1, n_ffn)), full((1, n_ffn)),
                  full((n_ffn, n_ffn)), full((1, n_ffn)), full((n_ffn, 2 * HY_ORDER * hy)),
                  pl.BlockSpec((n, hy), lambda l, i: (0, 0)),
                  pl.BlockSpec((tm, n), lambda l, i: (i, 0)),
                  pl.BlockSpec((tm, n), lambda l, i: (i, 0))],
        out_specs=(out_spec, out_spec, out_spec),
        scratch_shapes=[pltpu.VMEM((n, n_ffn), F32),
                        pltpu.VMEM((n, HY_ORDER * hy), BF16), pltpu.VMEM((n, HY_ORDER * hy), BF16),
                        pltpu.VMEM((8, HY_ORDER * hy), F32)],
        compiler_params=_cparams("parallel", "arbitrary"),
        name="hyena_filter",
    )(feat, pick(hf_w1, kp, fp), vec(hf_b1), vec(hf_freq), pick(hf_w2, fp, fp), vec(hf_b2),
      pick(hf_w3, fp, hf_w3.shape[2]), window, c_tab, s_tab)


def _shortconv_body(x_ref, w_ref, b_ref, o_ref):
    x = x_ref[...].astype(F32)
    n = x.shape[0]
    t = lax.broadcasted_iota(jnp.int32, x.shape, 0)
    prev = jnp.where(t == 0, 0.0, pltpu.roll(x, 1, 0))
    nxt = jnp.where(t == n - 1, 0.0, pltpu.roll(x, n - 1, 0))
    w = w_ref[...]
    o_ref[...] = b_ref[...] + w[0:1] * prev + w[1:2] * x + w[2:3] * nxt


def _shortconv(proj, col0, hy, short_w, short_b, n_seq, seq):
    assert col0 % hy == 0
    cb = col0 // hy
    return pl.pallas_call(
        _shortconv_body,
        out_shape=jax.ShapeDtypeStruct((n_seq * seq, 3 * hy), F32),
        grid=(n_seq, 3),
        in_specs=[pl.BlockSpec((seq, hy), lambda b, j: (b, cb + j)),
                  pl.BlockSpec((short_w.shape[0], hy), lambda b, j: (0, j)),
                  pl.BlockSpec((1, hy), lambda b, j: (0, j))],
        out_specs=pl.BlockSpec((seq, hy), lambda b, j: (b, j)),
        compiler_params=_cparams("parallel", "parallel"),
        name="hyena_shortconv",
    )(proj, short_w, short_b.reshape(1, -1))


def _hy_fwd_body(c_ref, s_ref, z_ref, p_ref, q_ref, r_ref, yr_ref, ys_ref):
    z = z_ref[...].astype(BF16)
    zr = jnp.dot(c_ref[...], z, preferred_element_type=F32)
    zs = jnp.dot(s_ref[...], z, preferred_element_type=F32)
    p, q, r = p_ref[0, 0], q_ref[0, 0], r_ref[0, 0]
    yr_ref[...] = (zr * p - zs * q).astype(yr_ref.dtype)
    ys_ref[...] = (zr * q + zs * r).astype(ys_ref.dtype)


def _hy_inv_body(c_ref, st_ref, yr_ref, ys_ref, gate_ref, z_ref, skip_ref, o_ref):
    y = (jnp.dot(c_ref[...], yr_ref[...], preferred_element_type=F32)
         + jnp.dot(st_ref[...], ys_ref[...], preferred_element_type=F32))
    o_ref[...] = gate_ref[...] * (y + skip_ref[0] * z_ref[...])


def _hyena_order(z_arr, z_cb, gate_arr, gate_cb, spec, li, order, skip, tabs, n_seq, seq, hy):
    c_tab, s_tab, st_tab = tabs
    p_arr, q_arr, r_arr = spec
    tm = _tile(seq, 512, 16)
    tiles = seq // tm
    tab_spec = pl.BlockSpec((tm, seq), lambda b, i: (i, 0))
    spec_spec = pl.BlockSpec((1, 1, tm, hy), lambda b, i: (li, order, i, 0))
    y_sd = jax.ShapeDtypeStruct((n_seq * seq, hy), BF16)
    yr, ys = pl.pallas_call(
        _hy_fwd_body,
        out_shape=(y_sd, y_sd),
        grid=(n_seq, tiles),
        in_specs=[tab_spec, tab_spec,
                  pl.BlockSpec((seq, hy), lambda b, i: (b, z_cb)),
                  spec_spec, spec_spec, spec_spec],
        out_specs=(pl.BlockSpec((tm, hy), lambda b, i: (b * tiles + i, 0)),) * 2,
        compiler_params=_cparams("parallel", "parallel"),
        name="hyena_fwd_dft",
    )(c_tab, s_tab, z_arr, p_arr, q_arr, r_arr)
    return pl.pallas_call(
        _hy_inv_body,
        out_shape=jax.ShapeDtypeStruct((n_seq * seq, hy), F32),
        grid=(n_seq, tiles),
        in_specs=[tab_spec, tab_spec,
                  pl.BlockSpec((seq, hy), lambda b, i: (b, 0)),
                  pl.BlockSpec((seq, hy), lambda b, i: (b, 0)),
                  pl.BlockSpec((tm, hy), lambda b, i: (b * tiles + i, gate_cb)),
                  pl.BlockSpec((tm, hy), lambda b, i: (b * tiles + i, z_cb)),
                  pl.BlockSpec((1, 1, hy), lambda b, i: (order, 0, 0))],
        out_specs=pl.BlockSpec((tm, hy), lambda b, i: (b * tiles + i, 0)),
        compiler_params=_cparams("parallel", "parallel"),
        name="hyena_inv_dft",
    )(c_tab, st_tab, yr, ys, gate_arr, z_arr, skip.reshape(HY_ORDER, 1, hy))


def _hyena(proj, col0, hy, short_w, short_b, spec, li, skip, tabs, n_seq, seq):
    u = _shortconv(proj, col0, hy, short_w, short_b, n_seq, seq)
    z = _hyena_order(u, 0, u, 1, spec, li, 0, skip, tabs, n_seq, seq, hy)
    return _hyena_order(z, 0, u, 2, spec, li, 1, skip, tabs, n_seq, seq, hy)


def _layer_norm(y, g, b):
    mu = jnp.mean(y, axis=-1, keepdims=True)
    yc = y - mu
    var = jnp.mean(yc * yc, axis=-1, keepdims=True)
    return yc * lax.rsqrt(var + EPS) * g + b


def _merge_body(oa_ref, ob_ref, oc_ref, x_ref, gt_ref, g_ref, w_ref, lg_ref, lb_ref, o_ref, mix_sc, *, alpha):
    col = 0
    for ref in (oa_ref, ob_ref, oc_ref):
        v = ref[...]
        wd = v.shape[1]
        vn = v * lax.rsqrt(jnp.mean(v * v, axis=-1, keepdims=True) + EPS)
        mix_sc[:, col:col + wd] = (vn * g_ref[:, col:col + wd]).astype(BF16)
        col += wd
    mix = jnp.dot(mix_sc[...], w_ref[...], preferred_element_type=F32)
    y = alpha * x_ref[...] + gt_ref[0, 0] * mix
    o_ref[...] = _layer_norm(y, lg_ref[...], lb_ref[...])


def _merge(oa, ob, oc, x, modtab, layer, seq, mod_row_of_tile, g_mix, w_out, ln_g, ln_b, alpha, *, tm_target=256):
    r, d = x.shape
    tm = _tile(seq, tm_target)
    tiles_per_seq = seq // tm
    mod_idx = lambda i: mod_row_of_tile(i // tiles_per_seq)
    row = lambda a: pl.BlockSpec((tm, a.shape[1]), lambda i: (i, 0))
    vec = pl.BlockSpec((1, d), lambda i: (0, 0))
    return pl.pallas_call(
        functools.partial(_merge_body, alpha=alpha),
        out_shape=jax.ShapeDtypeStruct((r, d), F32),
        grid=(r // tm,),
        in_specs=[row(oa), row(ob), row(oc), row(x),
                  pl.BlockSpec((1, 1, 1, d), lambda i: (layer, mod_idx(i), 0, 2)),
                  vec, pl.BlockSpec(w_out.shape, lambda i: (0, 0)), vec, vec],
        out_specs=pl.BlockSpec((tm, d), lambda i: (i, 0)),
        scratch_shapes=[pltpu.VMEM((tm, w_out.shape[0]), BF16)],
        compiler_params=_cparams("parallel"),
        name="merge_outproj_ln",
    )(oa, ob, oc, x, modtab, g_mix.reshape(1, -1), w_out, ln_g.reshape(1, -1), ln_b.reshape(1, -1))


def _mlp_body(x_ref, sh_ref, sc_ref, gt_ref, w1_ref, w2_ref, lg_ref, lb_ref, o_ref, u_sc, acc_sc, *, alpha):
    f = pl.program_id(1)

    @pl.when(f == 0)
    def _():
        u_sc[...] = (x_ref[...] * (1.0 + sc_ref[0, 0]) + sh_ref[0, 0]).astype(BF16)
        acc_sc[...] = jnp.zeros_like(acc_sc)

    h = jnp.maximum(jnp.dot(u_sc[...], w1_ref[...], preferred_element_type=F32), 0.0)
    acc_sc[...] += jnp.dot((h * h).astype(BF16), w2_ref[...], preferred_element_type=F32)

    @pl.when(f == pl.num_programs(1) - 1)
    def _():
        y = alpha * x_ref[...] + gt_ref[0, 0] * acc_sc[...]
        o_ref[...] = _layer_norm(y, lg_ref[...], lb_ref[...])


def _mlp(x, modtab, layer, seq, mod_row_of_tile, w1, w2, ln_g, ln_b, alpha, *, tm_target=512, tf_target=1024):
    r, d = x.shape
    dff = w1.shape[1]
    tm = _tile(seq, tm_target)
    tf = _tile(dff, tf_target, 128)
    tiles_per_seq = seq // tm
    mod_idx = lambda i: mod_row_of_tile(i // tiles_per_seq)
    mod = lambda k: pl.BlockSpec((1, 1, 1, d), lambda i, f: (layer, mod_idx(i), 0, k))
    vec = pl.BlockSpec((1, d), lambda i, f: (0, 0))
    return pl.pallas_call(
        functools.partial(_mlp_body, alpha=alpha),
        out_shape=jax.ShapeDtypeStruct((r, d), F32),
        grid=(r // tm, dff // tf),
        in_specs=[pl.BlockSpec((tm, d), lambda i, f: (i, 0)), mod(3), mod(4), mod(5),
                  pl.BlockSpec((d, tf), lambda i, f: (0, f)),
                  pl.BlockSpec((tf, d), lambda i, f: (f, 0)), vec, vec],
        out_specs=pl.BlockSpec((tm, d), lambda i, f: (i, 0)),
        scratch_shapes=[pltpu.VMEM((tm, d), BF16), pltpu.VMEM((tm, d), F32)],
        compiler_params=_cparams("parallel", "arbitrary"),
        name="mlp_ln",
    )(x, modtab, modtab, modtab, w1, w2, ln_g.reshape(1, -1), ln_b.reshape(1, -1))


@functools.lru_cache(maxsize=None)
def _rope_tables(n):
    t = np.arange(n)
    row = (t // GRID_W).astype(np.float64)
    col = (t % GRID_W).astype(np.float64)
    n_pairs_axis = HEAD_DIM // 4
    inv_freq = ROPE_THETA ** (-np.arange(n_pairs_axis, dtype=np.float64) / n_pairs_axis)
    ang = np.concatenate([row[:, None] * inv_freq[None, :], col[:, None] * inv_freq[None, :]], axis=-1)
    cos = np.repeat(np.cos(ang), 2, axis=-1)
    sin = np.stack([-np.sin(ang), np.sin(ang)], axis=-1).reshape(n, HEAD_DIM)
    return cos.astype(np.float32), sin.astype(np.float32)


def kernel(x, c, ctx, c_ctx, w_mod, b_mod, w_in, q_norm_g, k_norm_g, hy_short_w, hy_short_b, hf_w1, hf_b1,
           hf_freq, hf_w2, hf_b2, hf_w3, hy_bias, nat_rpb, g_mix, w_out, ln1_g, ln1_b, w1, w2, ln2_g, ln2_b):
    bsz, n_lat, d = x.shape
    n_ctx = ctx.shape[1]
    depth = w_mod.shape[0]
    n_mix_heads = d // HEAD_DIM
    a_heads = n_mix_heads // 2
    hy = (n_mix_heads // 4) * HEAD_DIM
    c_heads = n_mix_heads - a_heads - hy // HEAD_DIM
    a_w, kv_w, c_w = a_heads * HEAD_DIM, A_KV_HEADS * HEAD_DIM, c_heads * HEAD_DIM
    attn_w = a_w + 2 * kv_w
    hy_col0 = attn_w
    nat_col0 = attn_w + 3 * hy
    assert w_in.shape[2] == nat_col0 + 3 * c_w and n_lat % GRID_W == 0
    alpha = (2 * depth) ** 0.25
    scale = HEAD_DIM ** -0.5 * LOG2E
    rows = n_lat // GRID_W
    kr = min((nat_rpb.shape[2] + 1) // 2, rows)
    ub_rows, ub_starts, ub_pids, nat_patterns = _nat_plan(rows, kr)

    n_rows = -(-(bsz + 1) // 8) * 8
    cc = jnp.zeros((n_rows, d), F32).at[:bsz].set(c).at[bsz].set(c_ctx)
    modtab = _modulation(cc, w_mod, b_mod).reshape(depth, n_rows, 1, 6 * d)
    lat_mod = lambda b: b
    ctx_mod = lambda b: bsz

    cos, sin = _rope_tables(n_lat)
    tabs_lat = _dft_tables(n_lat)
    spec_lat = _filter_spectrum(n_lat, hf_w1, hf_b1, hf_freq, hf_w2, hf_b2, hf_w3, tabs_lat, list(range(depth)))
    if depth > 1:
        tabs_ctx = _dft_tables(n_ctx)
        spec_ctx = _filter_spectrum(n_ctx, hf_w1, hf_b1, hf_freq, hf_w2, hf_b2, hf_w3, tabs_ctx,
                                    list(range(depth - 1)))

    xl = x.reshape(bsz * n_lat, d)
    hc = ctx.reshape(bsz * n_ctx, d)
    for l in range(depth):
        keep_ctx = l < depth - 1
        w_attn = w_in[l, :, :attn_w].astype(BF16)
        w_hyp = w_in[l, :, hy_col0:nat_col0].astype(BF16)
        w_nat = w_in[l, :, nat_col0:].astype(BF16)
        gq = q_norm_g[l].reshape(1, HEAD_DIM)
        gk = k_norm_g[l].reshape(1, HEAD_DIM)
        bias = _nat_bias(nat_rpb[l], kr, ub_rows, tuple(nat_patterns))

        attn_kw = dict(mode="attn", n_q=a_heads, n_k=A_KV_HEADS, gq=gq, gk=gk, scale=scale)
        nat_kw = dict(mode="nat", n_q=c_heads, scale=scale)
        qkv = _inproj(xl, modtab, l, n_lat, lat_mod, w_attn, cos=cos, sin=sin, **attn_kw)
        cqkv = _inproj(hc, modtab, l, n_ctx, ctx_mod, w_attn, **attn_kw)
        hyp = _inproj(xl, modtab, l, n_lat, lat_mod, w_hyp, mode="plain")
        nat = _inproj(xl, modtab, l, n_lat, lat_mod, w_nat, **nat_kw)
        cnat = _inproj(hc, modtab, l, n_ctx, ctx_mod, w_nat, **nat_kw)

        o_a = _attention(qkv, 0, a_heads, A_KV_HEADS,
                         [(qkv, a_w, a_w + kv_w, n_lat), (cqkv, a_w, a_w + kv_w, n_ctx)], bsz, n_lat)
        o_b = _hyena(hyp, 0, hy, hy_short_w[l], hy_short_b[l], spec_lat, l, hy_bias[l], tabs_lat, bsz, n_lat)
        o_c = _natten(nat, cnat, bias, ub_starts, ub_pids, bsz, n_lat, n_ctx, c_heads)
        w_out_l = w_out[l].astype(BF16)
        xl_new = _merge(o_a, o_b, o_c, xl, modtab, l, n_lat, lat_mod, g_mix[l], w_out_l, ln1_g[l], ln1_b[l], alpha)

        w1_l = w1[l].astype(BF16)
        w2_l = w2[l].astype(BF16)
        if keep_ctx:
            chyp = _inproj(hc, modtab, l, n_ctx, ctx_mod, w_hyp, mode="plain")
            co_a = _attention(cqkv, 0, a_heads, A_KV_HEADS, [(cqkv, a_w, a_w + kv_w, n_ctx)], bsz, n_ctx)
            co_b = _hyena(chyp, 0, hy, hy_short_w[l], hy_short_b[l], spec_ctx, l, hy_bias[l], tabs_ctx, bsz, n_ctx)
            co_c = _attention(cnat, 0, c_heads, c_heads, [(cnat, c_w, 2 * c_w, n_ctx)], bsz, n_ctx)
            hc = _merge(co_a, co_b, co_c, hc, modtab, l, n_ctx, ctx_mod, g_mix[l], w_out_l, ln1_g[l], ln1_b[l], alpha)
            hc = _mlp(hc, modtab, l, bsz * n_ctx, ctx_mod, w1_l, w2_l, ln2_g[l], ln2_b[l], alpha)
        xl = _mlp(xl_new, modtab, l, n_lat, lat_mod, w1_l, w2_l, ln2_g[l], ln2_b[l], alpha)
    return xl.reshape(bsz, n_lat, d)
```

```python
import functools
import math

import jax
import jax.numpy as jnp
import numpy as np
from jax import lax
from jax.experimental import pallas as pl
from jax.experimental.pallas import tpu as pltpu

F32 = jnp.float32
BF16 = jnp.bfloat16

HEAD_DIM = 128
GRID_W = 64
A_KV_HEADS = 2
ROPE_THETA = 10000.0
HY_ORDER = 2
HY_DECAY_TARGET = 1e-2
HY_DECAY_SHORT_PCT = 0.3
HY_DECAY_LONG_PCT = 1.5
HY_DECAY_SHIFT = 0.05
EPS = 1e-6
NEG_INF = -1e30
LOG2E = math.log2(math.e)
VMEM_LIMIT_BYTES = 56 * 1024 * 1024


def _cparams(*sem):
    return pltpu.CompilerParams(dimension_semantics=sem, vmem_limit_bytes=VMEM_LIMIT_BYTES)


def _tile(n, target, mult=8):
    if n <= target:
        return n
    t = (target // mult) * mult
    while t >= mult:
        if n % t == 0:
            return t
        t -= mult
    return n


def _mod_body(c_ref, w_ref, b_ref, o_ref):
    c = c_ref[...]
    s = (c / (1.0 + jnp.exp(-c))).astype(BF16)
    o_ref[0] = jnp.dot(s, w_ref[0].astype(BF16), preferred_element_type=F32) + b_ref[0]


def _modulation(cc, w_mod, b_mod):
    nl, d, n6 = w_mod.shape
    r = cc.shape[0]
    tn = _tile(n6, 1536, 128)
    return pl.pallas_call(
        _mod_body,
        out_shape=jax.ShapeDtypeStruct((nl, r, n6), F32),
        grid=(nl, n6 // tn),
        in_specs=[pl.BlockSpec((r, d), lambda l, j: (0, 0)),
                  pl.BlockSpec((1, d, tn), lambda l, j: (l, 0, j)),
                  pl.BlockSpec((1, 1, tn), lambda l, j: (l, 0, j))],
        out_specs=pl.BlockSpec((1, r, tn), lambda l, j: (l, 0, j)),
        compiler_params=_cparams("parallel", "parallel"),
        name="modulation",
    )(cc, w_mod, b_mod.reshape(nl, 1, n6))


def _rms_head(xh, g):
    return xh * lax.rsqrt(jnp.mean(xh * xh, axis=-1, keepdims=True) + EPS) * g


def _rope_head(y, cos, sin):
    lane = lax.broadcasted_iota(jnp.int32, y.shape, 1)
    swapped = jnp.where((lane & 1) == 0, pltpu.roll(y, HEAD_DIM - 1, 1), pltpu.roll(y, 1, 1))
    return y * cos + swapped * sin


def _inproj_body(*refs, mode, n_q, n_k, n_heads, rope, scale):
    x_ref, sh_ref, sc_ref, w_ref = refs[:4]
    o_ref = refs[-1]
    u = (x_ref[...] * (1.0 + sc_ref[0, 0]) + sh_ref[0, 0]).astype(BF16)
    acc = jnp.dot(u, w_ref[...], preferred_element_type=F32)
    if mode == "plain":
        o_ref[...] = acc.astype(o_ref.dtype)
        return
    if mode == "nat":
        for hb in range(n_heads):
            xh = acc[:, hb * HEAD_DIM:(hb + 1) * HEAD_DIM]
            if hb < n_q:
                xh = xh * scale
            o_ref[:, hb * HEAD_DIM:(hb + 1) * HEAD_DIM] = xh.astype(o_ref.dtype)
        return
    gq_ref, gk_ref = refs[4:6]
    gq = gq_ref[...] * scale
    gk = gk_ref[...]
    if rope:
        cos = refs[6][...]
        sin = refs[7][...]
    for hb in range(n_heads):
        xh = acc[:, hb * HEAD_DIM:(hb + 1) * HEAD_DIM]
        if hb < n_q + n_k:
            xh = _rms_head(xh, gq if hb < n_q else gk)
            if rope:
                xh = _rope_head(xh, cos, sin)
        o_ref[:, hb * HEAD_DIM:(hb + 1) * HEAD_DIM] = xh.astype(o_ref.dtype)


def _inproj(x, modtab, layer, seq, mod_row_of_tile, w, *, mode, n_q=0, n_k=0,
            gq=None, gk=None, cos=None, sin=None, scale=1.0, tm_target=512):
    r, d = x.shape
    width = w.shape[1]
    tm = _tile(seq, tm_target)
    tiles_per_seq = seq // tm
    rope = cos is not None
    mod_idx = lambda i: mod_row_of_tile(i // tiles_per_seq)
    in_specs = [pl.BlockSpec((tm, d), lambda i: (i, 0)),
                pl.BlockSpec((1, 1, 1, d), lambda i: (layer, mod_idx(i), 0, 0)),
                pl.BlockSpec((1, 1, 1, d), lambda i: (layer, mod_idx(i), 0, 1)),
                pl.BlockSpec((d, width), lambda i: (0, 0))]
    args = [x, modtab, modtab, w]
    if mode == "attn":
        in_specs += [pl.BlockSpec((1, HEAD_DIM), lambda i: (0, 0))] * 2
        args += [gq, gk]
        if rope:
            in_specs += [pl.BlockSpec((tm, HEAD_DIM), lambda i: (i % tiles_per_seq, 0))] * 2
            args += [cos, sin]
    body = functools.partial(_inproj_body, mode=mode, n_q=n_q, n_k=n_k, n_heads=width // HEAD_DIM,
                             rope=rope, scale=scale)
    return pl.pallas_call(
        body,
        out_shape=jax.ShapeDtypeStruct((r, width), BF16),
        grid=(r // tm,),
        in_specs=in_specs,
        out_specs=pl.BlockSpec((tm, width), lambda i: (i, 0)),
        compiler_params=_cparams("parallel"),
        name="inproj_" + mode,
    )(*args)


def _softmax_pv_t(s, vs):
    m = jnp.max(s[0], axis=0, keepdims=True)
    for si in s[1:]:
        m = jnp.maximum(m, jnp.max(si, axis=0, keepdims=True))
    p = [jnp.exp2(si - m) for si in s]
    l = jnp.sum(p[0], axis=0, keepdims=True)
    for pi in p[1:]:
        l = l + jnp.sum(pi, axis=0, keepdims=True)
    ot = None
    for v, pi in zip(vs, p):
        part = lax.dot_general(v, pi.astype(BF16), (((0,), (0,)), ((), ())), preferred_element_type=F32)
        ot = part if ot is None else ot + part
    return ot * (1.0 / l)


def _attn_body(*refs, g, n_src, n_sub):
    q_ref = refs[0]
    k_refs = refs[1:1 + n_src]
    v_refs = refs[1 + n_src:1 + 2 * n_src]
    o_ref = refs[-1]
    tq = q_ref.shape[0]
    hs = g // n_sub
    scores = []
    for j in range(n_sub):
        heads = range(j * hs, (j + 1) * hs)
        qs = jnp.concatenate([q_ref[:, h * HEAD_DIM:(h + 1) * HEAD_DIM] for h in heads], axis=0)
        scores.append([lax.dot_general(k_ref[...], qs, (((1,), (1,)), ((), ())), preferred_element_type=F32)
                       for k_ref in k_refs])
    for j in range(n_sub):
        ot = _softmax_pv_t(scores[j], [v_ref[...] for v_ref in v_refs])
        for i in range(hs):
            h = j * hs + i
            o_ref[:, h * HEAD_DIM:(h + 1) * HEAD_DIM] = ot[:, i * tq:(i + 1) * tq].T.astype(o_ref.dtype)


def _attention(qarr, q_col0, n_q_heads, n_kv_heads, srcs, n_seq, seq_q, *, tq_target=256):
    g = n_q_heads // n_kv_heads
    gw = g * HEAD_DIM
    assert q_col0 % gw == 0
    tq = _tile(seq_q, tq_target, 128)
    tiles = seq_q // tq
    q_cb = q_col0 // gw
    in_specs = [pl.BlockSpec((tq, gw), lambda b, h, t: (b * tiles + t, q_cb + h))]
    args = [qarr]
    for which in (1, 2):
        for src in srcs:
            arr, col0, nk = src[0], src[which], src[3]
            cb = col0 // HEAD_DIM
            in_specs.append(pl.BlockSpec((nk, HEAD_DIM), lambda b, h, t, cb=cb: (b, cb + h)))
            args.append(arr)
    return pl.pallas_call(
        functools.partial(_attn_body, g=g, n_src=len(srcs), n_sub=g),
        out_shape=jax.ShapeDtypeStruct((n_seq * seq_q, n_q_heads * HEAD_DIM), F32),
        grid=(n_seq, n_kv_heads, tiles),
        in_specs=in_specs,
        out_specs=pl.BlockSpec((tq, gw), lambda b, h, t: (b * tiles + t, h)),
        compiler_params=_cparams("parallel", "parallel", "parallel"),
        name="attention",
    )(*args)


NAT_GROUP_ROWS = 4


def _nat_plan(rows, kr):
    ub_rows = min(kr + NAT_GROUP_ROWS, rows)
    assert rows % NAT_GROUP_ROWS == 0
    starts, pids, patterns = [], [], []
    for r0 in range(0, rows, NAT_GROUP_ROWS):
        band = lambda r: min(max(r - kr // 2, 0), rows - kr)
        ub = min(band(r0), rows - ub_rows)
        pat = tuple((band(r) - ub, r - ub) for r in range(r0, r0 + NAT_GROUP_ROWS))
        if pat not in patterns:
            patterns.append(pat)
        starts.append(ub)
        pids.append(patterns.index(pat))
    return ub_rows, starts, pids, patterns


def _nat_bias_body(rpb_ref, o_ref, *, win_r, win_c, kr, ub_rows, patterns):
    h = pl.program_id(0)
    w = GRID_W
    ck = lax.broadcasted_iota(jnp.int32, (w, w), 0)
    cq = lax.broadcasted_iota(jnp.int32, (w, w), 1)
    c_start = jnp.clip(cq - win_c // 2, 0, w - win_c)
    inside = (ck >= c_start) & (ck < c_start + win_c)
    d_col = ck - cq + (win_c - 1)
    n_dr = 2 * win_r - 1
    n_dc = 2 * win_c - 1
    tiles = []
    for dr in range(n_dr):
        t = jnp.zeros((w, w), F32)
        for dc in range(n_dc):
            t = jnp.where(d_col == dc, rpb_ref[(h * n_dr + dr) * n_dc + dc], t)
        tiles.append(jnp.where(inside, t * LOG2E, NEG_INF))
    masked = jnp.full((w, w), NEG_INF, F32)
    for p, pat in enumerate(patterns):
        for rq, (s_rel, r_rel) in enumerate(pat):
            for kk in range(ub_rows):
                blk = tiles[kk - r_rel + win_r - 1] if s_rel <= kk < s_rel + kr else masked
                o_ref[0, p, kk * w:(kk + 1) * w, rq * w:(rq + 1) * w] = blk


def _nat_bias(rpb, kr, ub_rows, patterns):
    nh, n_dr, n_dc = rpb.shape
    win_r, win_c = (n_dr + 1) // 2, (n_dc + 1) // 2
    shape = (len(patterns), ub_rows * GRID_W, NAT_GROUP_ROWS * GRID_W)
    return pl.pallas_call(
        functools.partial(_nat_bias_body, win_r=win_r, win_c=win_c, kr=kr, ub_rows=ub_rows, patterns=patterns),
        out_shape=jax.ShapeDtypeStruct((nh,) + shape, F32),
        grid=(nh,),
        in_specs=[pl.BlockSpec(memory_space=pltpu.SMEM)],
        out_specs=pl.BlockSpec((1,) + shape, lambda h: (h, 0, 0, 0)),
        compiler_params=_cparams("parallel"),
        name="nat_bias",
    )(rpb.reshape(-1))


def _nat_body(q_ref, k_ref, v_ref, kc_ref, vc_ref, bias_ref, o_ref, *, starts, pids):
    w = GRID_W
    gq = NAT_GROUP_ROWS * w
    nk = bias_ref.shape[2]
    nt = (((1,), (1,)), ((), ()))
    scores = []
    for g, (ub, pid) in enumerate(zip(starts, pids)):
        q = q_ref[g * gq:(g + 1) * gq, :]
        s_loc = lax.dot_general(k_ref[ub * w:ub * w + nk, :], q, nt, preferred_element_type=F32)
        s_ctx = lax.dot_general(kc_ref[...], q, nt, preferred_element_type=F32)
        scores.append([s_loc + bias_ref[0, pid], s_ctx])
    for g, ub in enumerate(starts):
        ot = _softmax_pv_t(scores[g], [v_ref[ub * w:ub * w + nk, :], vc_ref[...]])
        o_ref[g * gq:(g + 1) * gq, :] = ot.T.astype(o_ref.dtype)


def _natten(nat, cnat, bias, starts, pids, n_seq, seq, n_ctx, n_heads):
    blk = lambda part: pl.BlockSpec((seq, HEAD_DIM), lambda h, b: (b, part * n_heads + h))
    cblk = lambda part: pl.BlockSpec((n_ctx, HEAD_DIM), lambda h, b: (b, part * n_heads + h))
    return pl.pallas_call(
        functools.partial(_nat_body, starts=tuple(starts), pids=tuple(pids)),
        out_shape=jax.ShapeDtypeStruct((n_seq * seq, n_heads * HEAD_DIM), F32),
        grid=(n_heads, n_seq),
        in_specs=[blk(0), blk(1), blk(2), cblk(1), cblk(2),
                  pl.BlockSpec((1,) + bias.shape[1:], lambda h, b: (h, 0, 0, 0))],
        out_specs=pl.BlockSpec((seq, HEAD_DIM), lambda h, b: (b, h)),
        compiler_params=_cparams("parallel", "parallel"),
        name="natten",
    )(nat, nat, nat, cnat, cnat, bias)


LANES = 128


@functools.lru_cache(maxsize=None)
def _dft_tables(n):
    h = n // 2
    k = np.arange(h, dtype=np.int64)
    even = ((k[:, None] * (2 * k[None, :])) % (2 * n)).astype(np.float64) * (math.pi / n)
    odd = ((k[:, None] * (2 * k[None, :] + 1)) % (2 * n)).astype(np.float64) * (math.pi / n)
    to_bf16 = lambda a: np.asarray(a, np.float32).astype(BF16)
    return tuple(to_bf16(a) for a in (np.cos(even), np.sin(even), np.cos(odd), np.sin(odd),
                                      np.cos(odd).T, np.sin(odd).T))


@functools.lru_cache(maxsize=None)
def _filter_consts(n, n_bands, hy):
    t = np.linspace(0.0, 1.0, n)
    bands = np.arange(1, n_bands + 1, dtype=np.float64)
    ang = 2.0 * math.pi * t[:, None] * bands[None, :]
    feat = np.concatenate([t[:, None], np.cos(ang), np.sin(ang)], axis=-1)
    max_decay = math.log(HY_DECAY_TARGET) / HY_DECAY_SHORT_PCT
    min_decay = math.log(HY_DECAY_TARGET) / HY_DECAY_LONG_PCT
    deltas = np.abs(np.linspace(min_decay, max_decay, hy))
    window = np.exp(-t[:, None] * deltas[None, :]) + HY_DECAY_SHIFT
    return feat.astype(np.float32), window.astype(np.float32)


def _to_slabs(slab_ref, x):
    for s in range(slab_ref.shape[0]):
        slab_ref[s] = x[:, s * LANES:(s + 1) * LANES]


def _even_odd(slab_ref):
    ns, n, _ = slab_ref.shape
    cat = lambda parts: parts[0] if ns == 1 else jnp.concatenate(parts, axis=1)
    ev = cat([slab_ref[s, pl.ds(0, n // 2, stride=2), :] for s in range(ns)])
    od = cat([slab_ref[s, pl.ds(1, n // 2, stride=2), :] for s in range(ns)])
    return ev, od


def _alt_sign(h):
    j = lax.broadcasted_iota(jnp.int32, (h, 1), 0)
    return jnp.where((j & 1) == 0, 1.0, -1.0)


def _dot(a, b):
    return jnp.dot(a, b, preferred_element_type=F32)


def _filter_body(feat_ref, w1_ref, b1_ref, fr_ref, w2_ref, b2_ref, w3_ref, win_ref,
                 ce_ref, se_ref, co_ref, so_ref, spec_ref, mid_ref, hhi_sc, hlo_sc, ge_sc, go_sc, *, n, hy):
    j = pl.program_id(1)
    hp = lax.Precision.HIGHEST
    half = n // 2
    cw = win_ref.shape[1]

    @pl.when(j == 0)
    def _():
        fr = fr_ref[0]
        hid = jnp.sin(fr * (jnp.dot(feat_ref[...], w1_ref[0], precision=hp, preferred_element_type=F32) + b1_ref[0]))
        hid = jnp.sin(fr * (jnp.dot(hid, w2_ref[0], precision=hp, preferred_element_type=F32) + b2_ref[0]))
        hhi_sc[...] = hid.astype(BF16)
        hlo_sc[...] = (hid - hid.astype(BF16).astype(F32)).astype(BF16)

    def last_layer(w):
        w_hi = w.astype(BF16)
        w_lo = (w - w_hi.astype(F32)).astype(BF16)
        return _dot(hhi_sc[...], w_hi) + (_dot(hhi_sc[...], w_lo) + _dot(hlo_sc[...], w_hi))

    win = win_ref[...]
    t_idx = lax.broadcasted_iota(jnp.int32, (n, cw), 0)
    alt = _alt_sign(half)
    k_idx = lax.broadcasted_iota(jnp.int32, (half, cw), 0)
    wk = jnp.where(k_idx == 0, 1.0 / (2 * n), 2.0 / (2 * n))
    for o in range(HY_ORDER):
        fwd = last_layer(w3_ref[0, 2 * o]) * win
        bwd = last_layer(w3_ref[0, 2 * o + 1]) * win
        bwd = jnp.where(t_idx == 0, 0.0, bwd)
        inv = 1.0 / (jnp.sum(jnp.abs(fwd), axis=0, keepdims=True)
                     + jnp.sum(jnp.abs(bwd), axis=0, keepdims=True) + EPS)
        _to_slabs(ge_sc, (fwd + bwd) * inv)
        _to_slabs(go_sc, (fwd - bwd) * inv)
        ge_e, ge_o = _even_odd(ge_sc)
        go_e, go_o = _even_odd(go_sc)
        ac = _dot(ce_ref[...], ge_e.astype(BF16))
        bc = _dot(co_ref[...], ge_o.astype(BF16))
        a_s = _dot(se_ref[...], go_e.astype(BF16))
        bs = _dot(so_ref[...], go_o.astype(BF16))
        spec_ref[0, o, 0] = (ac + bc) * wk
        spec_ref[0, o, 1] = (a_s + bs) * wk
        spec_ref[0, o, 2] = (ac - bc) * wk
        spec_ref[0, o, 3] = (bs - a_s) * wk
        mid_r = jnp.sum(ge_e * alt, axis=0, keepdims=True) * (2.0 / (2 * n))
        mid_s = jnp.sum(go_o * alt, axis=0, keepdims=True) * (2.0 / (2 * n))
        row = lax.broadcasted_iota(jnp.int32, (8, cw), 0)
        mid_ref[0, o] = jnp.where(row == 0, mid_r, jnp.where(row == 1, mid_s, 0.0))


def _filter_spectrum(n, hf_w1, hf_b1, hf_freq, hf_w2, hf_b2, hf_w3, tabs, layers):
    nl = len(layers)
    n_pos, n_ffn = hf_w1.shape[1:]
    hy = hf_w3.shape[2] // (2 * HY_ORDER)
    half = n // 2
    cw = LANES
    feat, window = _filter_consts(n, (n_pos - 1) // 2, hy)
    lsel = jnp.asarray(layers, jnp.int32)
    lanes = lambda k: -(-k // LANES) * LANES
    kp, fp = lanes(n_pos), lanes(n_ffn)
    feat = np.pad(feat, ((0, 0), (0, kp - n_pos)))
    pick = lambda a, r, c: jnp.pad(a[lsel], ((0, 0), (0, r - a.shape[1]), (0, c - a.shape[2])))
    vec = lambda a: jnp.pad(a[lsel], ((0, 0), (0, fp - a.shape[1]))).reshape(nl, 1, fp)
    w3 = pick(hf_w3, fp, hf_w3.shape[2]).reshape(nl, fp, 2 * HY_ORDER, hy).transpose(0, 2, 1, 3)
    full = lambda shp: pl.BlockSpec((1,) + shp, lambda l, j: (l, 0, 0))
    tab = pl.BlockSpec((half, half), lambda l, j: (0, 0))
    return pl.pallas_call(
        functools.partial(_filter_body, n=n, hy=hy),
        out_shape=(jax.ShapeDtypeStruct((nl, HY_ORDER, 4, half, hy), F32),
                   jax.ShapeDtypeStruct((nl, HY_ORDER, 8, hy), F32)),
        grid=(nl, hy // cw),
        in_specs=[pl.BlockSpec((n, kp), lambda l, j: (0, 0)),
                  full((kp, fp)), full((1, fp)), full((1, fp)), full((fp, fp)), full((1, fp)),
                  pl.BlockSpec((1, 2 * HY_ORDER, fp, cw), lambda l, j: (l, 0, 0, j)),
                  pl.BlockSpec((n, cw), lambda l, j: (0, j)),
                  tab, tab, tab, tab],
        out_specs=(pl.BlockSpec((1, HY_ORDER, 4, half, cw), lambda l, j: (l, 0, 0, 0, j)),
                   pl.BlockSpec((1, HY_ORDER, 8, cw), lambda l, j: (l, 0, 0, j))),
        scratch_shapes=[pltpu.VMEM((n, fp), BF16), pltpu.VMEM((n, fp), BF16),
                        pltpu.VMEM((cw // LANES, n, LANES), F32), pltpu.VMEM((cw // LANES, n, LANES), F32)],
        compiler_params=_cparams("parallel", "arbitrary"),
        name="hyena_filter",
    )(feat, pick(hf_w1, kp, fp), vec(hf_b1), vec(hf_freq), pick(hf_w2, fp, fp), vec(hf_b2), w3, window,
      *tabs[:4])


def _conv3(x, w_ref, b_ref):
    n = x.shape[0]
    t = lax.broadcasted_iota(jnp.int32, x.shape, 0)
    prev = jnp.where(t == 0, 0.0, pltpu.roll(x, 1, 0))
    nxt = jnp.where(t == n - 1, 0.0, pltpu.roll(x, n - 1, 0))
    return b_ref[...] + w_ref[0:1, :] * prev + w_ref[1:2, :] * x + w_ref[2:3, :] * nxt


def _hyena_body(*refs, conv_z):
    ce_ref, se_ref, co_ref, so_ref, cot_ref, sot_ref, z_ref, gate_ref = refs[:8]
    i = 8
    if conv_z:
        wz_ref, bz_ref = refs[i:i + 2]
        i += 2
    wg_ref, bg_ref, skip_ref, spec_ref, mid_ref, o_ref, z_sc, y_sc = refs[i:]
    half = ce_ref.shape[0]
    z = z_ref[...].astype(F32)
    if conv_z:
        z = _conv3(z, wz_ref, bz_ref)
    _to_slabs(z_sc, z)
    ze, zo = _even_odd(z_sc)
    ze16, zo16 = ze.astype(BF16), zo.astype(BF16)
    ac, bc = _dot(ce_ref[...], ze16), _dot(co_ref[...], zo16)
    a_s, bs = _dot(se_ref[...], ze16), _dot(so_ref[...], zo16)
    alt = _alt_sign(half)

    def cmul(zr, zs, hr, hs):
        return zr * hr - zs * hs, zr * hs + zs * hr

    yr_lo, ys_lo = cmul(ac + bc, a_s + bs, spec_ref[0, 0, 0], spec_ref[0, 0, 1])
    yr_hi, ys_hi = cmul(ac - bc, bs - a_s, spec_ref[0, 0, 2], spec_ref[0, 0, 3])
    yr_mid, ys_mid = cmul(jnp.sum(ze * alt, axis=0, keepdims=True), jnp.sum(zo * alt, axis=0, keepdims=True),
                          mid_ref[0, 0, 0:1, :], mid_ref[0, 0, 1:2, :])
    y_e = (_dot(ce_ref[...], (yr_lo + yr_hi).astype(BF16)) + _dot(se_ref[...], (ys_lo - ys_hi).astype(BF16))
           + alt * yr_mid)
    y_o = (_dot(cot_ref[...], (yr_lo - yr_hi).astype(BF16)) + _dot(sot_ref[...], (ys_lo + ys_hi).astype(BF16))
           + alt * ys_mid)
    skip = skip_ref[0]
    y_e = y_e + skip * ze
    y_o = y_o + skip * zo
    for s in range(y_sc.shape[0]):
        y_sc[s, pl.ds(0, half, stride=2), :] = y_e[:, s * LANES:(s + 1) * LANES]
        y_sc[s, pl.ds(1, half, stride=2), :] = y_o[:, s * LANES:(s + 1) * LANES]
    gate = _conv3(gate_ref[...].astype(F32), wg_ref, bg_ref)
    for s in range(y_sc.shape[0]):
        o_ref[:, s * LANES:(s + 1) * LANES] = gate[:, s * LANES:(s + 1) * LANES] * y_sc[s]


def _hyena_order(z_arr, z_part, hyp, gate_part, short_w, short_b, spec, mid, li, order, skip, tabs,
                 n_seq, seq, hy, *, conv_z):
    half = seq // 2
    cw = min(hy, 2 * LANES)
    cpb = hy // cw
    tab = pl.BlockSpec((half, half), lambda b, j: (0, 0))
    part = lambda p, rows: pl.BlockSpec((rows, cw), lambda b, j: (0, p * cpb + j))
    in_specs = [tab] * 6 + [pl.BlockSpec((seq, cw), lambda b, j: (b, z_part * cpb + j)),
                            pl.BlockSpec((seq, cw), lambda b, j: (b, gate_part * cpb + j))]
    args = list(tabs) + [z_arr, hyp]
    if conv_z:
        in_specs += [part(z_part, short_w.shape[0]), part(z_part, 1)]
        args += [short_w, short_b.reshape(1, -1)]
    in_specs += [part(gate_part, short_w.shape[0]), part(gate_part, 1),
                 pl.BlockSpec((1, 1, cw), lambda b, j: (order, 0, j)),
                 pl.BlockSpec((1, 1, 4, half, cw), lambda b, j: (li, order, 0, 0, j)),
                 pl.BlockSpec((1, 1, 8, cw), lambda b, j: (li, order, 0, j))]
    args += [short_w, short_b.reshape(1, -1), skip.reshape(HY_ORDER, 1, hy), spec, mid]
    return pl.pallas_call(
        functools.partial(_hyena_body, conv_z=conv_z),
        out_shape=jax.ShapeDtypeStruct((n_seq * seq, hy), F32),
        grid=(n_seq, cpb),
        in_specs=in_specs,
        out_specs=pl.BlockSpec((seq, cw), lambda b, j: (b, j)),
        scratch_shapes=[pltpu.VMEM((cw // LANES, seq, LANES), F32), pltpu.VMEM((cw // LANES, seq, LANES), F32)],
        compiler_params=_cparams("parallel", "parallel"),
        name="hyena_order",
    )(*args)


def _hyena(hyp, hy, short_w, short_b, spec, mid, li, skip, tabs, n_seq, seq):
    kw = dict(short_w=short_w, short_b=short_b, spec=spec, mid=mid, li=li, skip=skip, tabs=tabs,
              n_seq=n_seq, seq=seq, hy=hy)
    z = _hyena_order(hyp, 0, hyp, 1, order=0, conv_z=True, **kw)
    return _hyena_order(z, 0, hyp, 2, order=1, conv_z=False, **kw)


def _layer_norm(y, g, b):
    mu = jnp.mean(y, axis=-1, keepdims=True)
    yc = y - mu
    var = jnp.mean(yc * yc, axis=-1, keepdims=True)
    return yc * lax.rsqrt(var + EPS) * g + b


def _merge_body(oa_ref, ob_ref, oc_ref, x_ref, gt_ref, g_ref, w_ref, lg_ref, lb_ref, o_ref, mix_sc, *, alpha):
    col = 0
    for ref in (oa_ref, ob_ref, oc_ref):
        v = ref[...]
        wd = v.shape[1]
        vn = v * lax.rsqrt(jnp.mean(v * v, axis=-1, keepdims=True) + EPS)
        mix_sc[:, col:col + wd] = (vn * g_ref[:, col:col + wd]).astype(BF16)
        col += wd
    mix = jnp.dot(mix_sc[...], w_ref[...], preferred_element_type=F32)
    y = alpha * x_ref[...] + gt_ref[0, 0] * mix
    o_ref[...] = _layer_norm(y, lg_ref[...], lb_ref[...])


def _merge(oa, ob, oc, x, modtab, layer, seq, mod_row_of_tile, g_mix, w_out, ln_g, ln_b, alpha, *, tm_target=256):
    r, d = x.shape
    tm = _tile(seq, tm_target)
    tiles_per_seq = seq // tm
    mod_idx = lambda i: mod_row_of_tile(i // tiles_per_seq)
    row = lambda a: pl.BlockSpec((tm, a.shape[1]), lambda i: (i, 0))
    vec = pl.BlockSpec((1, d), lambda i: (0, 0))
    return pl.pallas_call(
        functools.partial(_merge_body, alpha=alpha),
        out_shape=jax.ShapeDtypeStruct((r, d), F32),
        grid=(r // tm,),
        in_specs=[row(oa), row(ob), row(oc), row(x),
                  pl.BlockSpec((1, 1, 1, d), lambda i: (layer, mod_idx(i), 0, 2)),
                  vec, pl.BlockSpec(w_out.shape, lambda i: (0, 0)), vec, vec],
        out_specs=pl.BlockSpec((tm, d), lambda i: (i, 0)),
        scratch_shapes=[pltpu.VMEM((tm, w_out.shape[0]), BF16)],
        compiler_params=_cparams("parallel"),
        name="merge_outproj_ln",
    )(oa, ob, oc, x, modtab, g_mix.reshape(1, -1), w_out, ln_g.reshape(1, -1), ln_b.reshape(1, -1))


def _mlp_body(x_ref, sh_ref, sc_ref, gt_ref, w1_ref, w2_ref, lg_ref, lb_ref, o_ref, u_sc, acc_sc, *, alpha):
    f = pl.program_id(1)

    @pl.when(f == 0)
    def _():
        u_sc[...] = (x_ref[...] * (1.0 + sc_ref[0, 0]) + sh_ref[0, 0]).astype(BF16)
        acc_sc[...] = jnp.zeros_like(acc_sc)

    h = jnp.maximum(jnp.dot(u_sc[...], w1_ref[...], preferred_element_type=F32), 0.0)
    acc_sc[...] += jnp.dot((h * h).astype(BF16), w2_ref[...], preferred_element_type=F32)

    @pl.when(f == pl.num_programs(1) - 1)
    def _():
        y = alpha * x_ref[...] + gt_ref[0, 0] * acc_sc[...]
        o_ref[...] = _layer_norm(y, lg_ref[...], lb_ref[...])


def _mlp(x, modtab, layer, seq, mod_row_of_tile, w1, w2, ln_g, ln_b, alpha, *, tm_target=512, tf_target=1024):
    r, d = x.shape
    dff = w1.shape[1]
    tm = _tile(seq, tm_target)
    tf = _tile(dff, tf_target, 128)
    tiles_per_seq = seq // tm
    mod_idx = lambda i: mod_row_of_tile(i // tiles_per_seq)
    mod = lambda k: pl.BlockSpec((1, 1, 1, d), lambda i, f: (layer, mod_idx(i), 0, k))
    vec = pl.BlockSpec((1, d), lambda i, f: (0, 0))
    return pl.pallas_call(
        functools.partial(_mlp_body, alpha=alpha),
        out_shape=jax.ShapeDtypeStruct((r, d), F32),
        grid=(r // tm, dff // tf),
        in_specs=[pl.BlockSpec((tm, d), lambda i, f: (i, 0)), mod(3), mod(4), mod(5),
                  pl.BlockSpec((d, tf), lambda i, f: (0, f)),
                  pl.BlockSpec((tf, d), lambda i, f: (f, 0)), vec, vec],
        out_specs=pl.BlockSpec((tm, d), lambda i, f: (i, 0)),
        scratch_shapes=[pltpu.VMEM((tm, d), BF16), pltpu.VMEM((tm, d), F32)],
        compiler_params=_cparams("parallel", "arbitrary"),
        name="mlp_ln",
    )(x, modtab, modtab, modtab, w1, w2, ln_g.reshape(1, -1), ln_b.reshape(1, -1))


@functools.lru_cache(maxsize=None)
def _rope_tables(n):
    t = np.arange(n)
    row = (t // GRID_W).astype(np.float64)
    col = (t % GRID_W).astype(np.float64)
    n_pairs_axis = HEAD_DIM // 4
    inv_freq = ROPE_THETA ** (-np.arange(n_pairs_axis, dtype=np.float64) / n_pairs_axis)
    ang = np.concatenate([row[:, None] * inv_freq[None, :], col[:, None] * inv_freq[None, :]], axis=-1)
    cos = np.repeat(np.cos(ang), 2, axis=-1)
    sin = np.stack([-np.sin(ang), np.sin(ang)], axis=-1).reshape(n, HEAD_DIM)
    return cos.astype(np.float32), sin.astype(np.float32)


def kernel(x, c, ctx, c_ctx, w_mod, b_mod, w_in, q_norm_g, k_norm_g, hy_short_w, hy_short_b, hf_w1, hf_b1,
           hf_freq, hf_w2, hf_b2, hf_w3, hy_bias, nat_rpb, g_mix, w_out, ln1_g, ln1_b, w1, w2, ln2_g, ln2_b):
    bsz, n_lat, d = x.shape
    n_ctx = ctx.shape[1]
    depth = w_mod.shape[0]
    n_mix_heads = d // HEAD_DIM
    a_heads = n_mix_heads // 2
    hy = (n_mix_heads // 4) * HEAD_DIM
    c_heads = n_mix_heads - a_heads - hy // HEAD_DIM
    a_w, kv_w, c_w = a_heads * HEAD_DIM, A_KV_HEADS * HEAD_DIM, c_heads * HEAD_DIM
    attn_w = a_w + 2 * kv_w
    hy_col0 = attn_w
    nat_col0 = attn_w + 3 * hy
    assert w_in.shape[2] == nat_col0 + 3 * c_w and n_lat % GRID_W == 0
    alpha = (2 * depth) ** 0.25
    scale = HEAD_DIM ** -0.5 * LOG2E
    rows = n_lat // GRID_W
    kr = min((nat_rpb.shape[2] + 1) // 2, rows)
    ub_rows, ub_starts, ub_pids, nat_patterns = _nat_plan(rows, kr)

    n_rows = -(-(bsz + 1) // 8) * 8
    cc = jnp.zeros((n_rows, d), F32).at[:bsz].set(c).at[bsz].set(c_ctx)
    modtab = _modulation(cc, w_mod, b_mod).reshape(depth, n_rows, 1, 6 * d)
    lat_mod = lambda b: b
    ctx_mod = lambda b: bsz

    cos, sin = _rope_tables(n_lat)
    tabs_lat = _dft_tables(n_lat)
    spec_lat, mid_lat = _filter_spectrum(n_lat, hf_w1, hf_b1, hf_freq, hf_w2, hf_b2, hf_w3, tabs_lat,
                                         list(range(depth)))
    if depth > 1:
        tabs_ctx = _dft_tables(n_ctx)
        spec_ctx, mid_ctx = _filter_spectrum(n_ctx, hf_w1, hf_b1, hf_freq, hf_w2, hf_b2, hf_w3, tabs_ctx,
                                             list(range(depth - 1)))

    xl = x.reshape(bsz * n_lat, d)
    hc = ctx.reshape(bsz * n_ctx, d)
    for l in range(depth):
        keep_ctx = l < depth - 1
        w_attn = w_in[l, :, :attn_w].astype(BF16)
        w_hyp = w_in[l, :, hy_col0:nat_col0].astype(BF16)
        w_nat = w_in[l, :, nat_col0:].astype(BF16)
        gq = q_norm_g[l].reshape(1, HEAD_DIM)
        gk = k_norm_g[l].reshape(1, HEAD_DIM)
        bias = _nat_bias(nat_rpb[l], kr, ub_rows, tuple(nat_patterns))

        attn_kw = dict(mode="attn", n_q=a_heads, n_k=A_KV_HEADS, gq=gq, gk=gk, scale=scale)
        nat_kw = dict(mode="nat", n_q=c_heads, scale=scale)
        qkv = _inproj(xl, modtab, l, n_lat, lat_mod, w_attn, cos=cos, sin=sin, **attn_kw)
        cqkv = _inproj(hc, modtab, l, n_ctx, ctx_mod, w_attn, **attn_kw)
        hyp = _inproj(xl, modtab, l, n_lat, lat_mod, w_hyp, mode="plain")
        nat = _inproj(xl, modtab, l, n_lat, lat_mod, w_nat, **nat_kw)
        cnat = _inproj(hc, modtab, l, n_ctx, ctx_mod, w_nat, **nat_kw)

        o_a = _attention(qkv, 0, a_heads, A_KV_HEADS,
                         [(qkv, a_w, a_w + kv_w, n_lat), (cqkv, a_w, a_w + kv_w, n_ctx)], bsz, n_lat)
        o_b = _hyena(hyp, hy, hy_short_w[l], hy_short_b[l], spec_lat, mid_lat, l, hy_bias[l], tabs_lat, bsz, n_lat)
        o_c = _natten(nat, cnat, bias, ub_starts, ub_pids, bsz, n_lat, n_ctx, c_heads)
        w_out_l = w_out[l].astype(BF16)
        xl_new = _merge(o_a, o_b, o_c, xl, modtab, l, n_lat, lat_mod, g_mix[l], w_out_l, ln1_g[l], ln1_b[l], alpha)

        w1_l = w1[l].astype(BF16)
        w2_l = w2[l].astype(BF16)
        if keep_ctx:
            chyp = _inproj(hc, modtab, l, n_ctx, ctx_mod, w_hyp, mode="plain")
            co_a = _attention(cqkv, 0, a_heads, A_KV_HEADS, [(cqkv, a_w, a_w + kv_w, n_ctx)], bsz, n_ctx)
            co_b = _hyena(chyp, hy, hy_short_w[l], hy_short_b[l], spec_ctx, mid_ctx, l, hy_bias[l], tabs_ctx, bsz, n_ctx)
            co_c = _attention(cnat, 0, c_heads, c_heads, [(cnat, c_w, 2 * c_w, n_ctx)], bsz, n_ctx)
            hc = _merge(co_a, co_b, co_c, hc, modtab, l, n_ctx, ctx_mod, g_mix[l], w_out_l, ln1_g[l], ln1_b[l], alpha)
            hc = _mlp(hc, modtab, l, bsz * n_ctx, ctx_mod, w1_l, w2_l, ln2_g[l], ln2_b[l], alpha)
        xl = _mlp(xl_new, modtab, l, n_lat, lat_mod, w1_l, w2_l, ln2_g[l], ln2_b[l], alpha)
    return xl.reshape(bsz, n_lat, d)
```

```python
import functools
import math

import jax
import jax.numpy as jnp
import numpy as np
from jax import lax
from jax.experimental import pallas as pl
from jax.experimental.pallas import tpu as pltpu

F32 = jnp.float32
BF16 = jnp.bfloat16

HEAD_DIM = 128
GRID_W = 64
A_KV_HEADS = 2
ROPE_THETA = 10000.0
HY_ORDER = 2
HY_DECAY_TARGET = 1e-2
HY_DECAY_SHORT_PCT = 0.3
HY_DECAY_LONG_PCT = 1.5
HY_DECAY_SHIFT = 0.05
EPS = 1e-6
NEG_INF = -1e30
LOG2E = math.log2(math.e)
VMEM_LIMIT_BYTES = 56 * 1024 * 1024


def _cparams(*sem):
    return pltpu.CompilerParams(dimension_semantics=sem, vmem_limit_bytes=VMEM_LIMIT_BYTES)


def _tile(n, target, mult=8):
    if n <= target:
        return n
    t = (target // mult) * mult
    while t >= mult:
        if n % t == 0:
            return t
        t -= mult
    return n


def _cast_body(w_ref, o_ref):
    o_ref[...] = w_ref[0].astype(o_ref.dtype)


def _to_bf16(w, layer):
    _, r, c = w.shape
    tr = _tile(r, max(8, (8 * 1024 * 1024) // (4 * c)), 16)
    return pl.pallas_call(
        _cast_body,
        out_shape=jax.ShapeDtypeStruct((r, c), BF16),
        grid=(r // tr,),
        in_specs=[pl.BlockSpec((1, tr, c), lambda i: (layer, i, 0))],
        out_specs=pl.BlockSpec((tr, c), lambda i: (i, 0)),
        compiler_params=_cparams("parallel"),
        name="weight_to_bf16",
    )(w)


def _mod_body(c_ref, w_ref, b_ref, o_ref):
    c = c_ref[...]
    s = (c / (1.0 + jnp.exp(-c))).astype(BF16)
    o_ref[0] = jnp.dot(s, w_ref[0].astype(BF16), preferred_element_type=F32) + b_ref[0]


def _modulation(cc, w_mod, b_mod):
    nl, d, n6 = w_mod.shape
    r = cc.shape[0]
    tn = _tile(n6, 1536, 128)
    return pl.pallas_call(
        _mod_body,
        out_shape=jax.ShapeDtypeStruct((nl, r, n6), F32),
        grid=(nl, n6 // tn),
        in_specs=[pl.BlockSpec((r, d), lambda l, j: (0, 0)),
                  pl.BlockSpec((1, d, tn), lambda l, j: (l, 0, j)),
                  pl.BlockSpec((1, 1, tn), lambda l, j: (l, 0, j))],
        out_specs=pl.BlockSpec((1, r, tn), lambda l, j: (l, 0, j)),
        compiler_params=_cparams("parallel", "parallel"),
        name="modulation",
    )(cc, w_mod, b_mod.reshape(nl, 1, n6))


def _rms_head(xh, g):
    return xh * lax.rsqrt(jnp.mean(xh * xh, axis=-1, keepdims=True) + EPS) * g


def _rope_head(y, cos, sin):
    lane = lax.broadcasted_iota(jnp.int32, y.shape, 1)
    swapped = jnp.where((lane & 1) == 0, pltpu.roll(y, HEAD_DIM - 1, 1), pltpu.roll(y, 1, 1))
    return y * cos + swapped * sin


def _inproj_body(*refs, n_q, n_k, plain_col0, nat_col0, n_nat_q, rope, scale):
    x_ref, sh_ref, sc_ref, w_ref, gq_ref, gk_ref = refs[:6]
    o_ref = refs[-1]
    width = w_ref.shape[1]
    u = (x_ref[...] * (1.0 + sc_ref[0, 0]) + sh_ref[0, 0]).astype(BF16)
    acc_attn = jnp.dot(u, w_ref[:, :plain_col0], preferred_element_type=F32)
    acc_plain = jnp.dot(u, w_ref[:, plain_col0:nat_col0], preferred_element_type=F32)
    acc_nat = jnp.dot(u, w_ref[:, nat_col0:], preferred_element_type=F32)
    gq = gq_ref[...] * scale
    gk = gk_ref[...]
    if rope:
        cos = refs[6][...]
        sin = refs[7][...]
    for hb in range(plain_col0 // HEAD_DIM):
        xh = acc_attn[:, hb * HEAD_DIM:(hb + 1) * HEAD_DIM]
        if hb < n_q + n_k:
            xh = _rms_head(xh, gq if hb < n_q else gk)
            if rope:
                xh = _rope_head(xh, cos, sin)
        o_ref[:, hb * HEAD_DIM:(hb + 1) * HEAD_DIM] = xh.astype(o_ref.dtype)
    o_ref[:, plain_col0:nat_col0] = acc_plain.astype(o_ref.dtype)
    for hb in range((width - nat_col0) // HEAD_DIM):
        xh = acc_nat[:, hb * HEAD_DIM:(hb + 1) * HEAD_DIM]
        if hb < n_nat_q:
            xh = xh * scale
        o_ref[:, nat_col0 + hb * HEAD_DIM:nat_col0 + (hb + 1) * HEAD_DIM] = xh.astype(o_ref.dtype)


def _inproj(x, modtab, layer, seq, mod_row_of_tile, w, gq, gk, *, n_q, n_k, plain_col0, nat_col0, n_nat_q,
            cos=None, sin=None, scale=1.0, tm_target=512):
    r, d = x.shape
    width = w.shape[1]
    tm = _tile(seq, tm_target)
    tiles_per_seq = seq // tm
    rope = cos is not None
    mod_idx = lambda i: mod_row_of_tile(i // tiles_per_seq)
    in_specs = [pl.BlockSpec((tm, d), lambda i: (i, 0)),
                pl.BlockSpec((1, 1, 1, d), lambda i: (layer, mod_idx(i), 0, 0)),
                pl.BlockSpec((1, 1, 1, d), lambda i: (layer, mod_idx(i), 0, 1)),
                pl.BlockSpec((d, width), lambda i: (0, 0)),
                pl.BlockSpec((1, HEAD_DIM), lambda i: (0, 0)), pl.BlockSpec((1, HEAD_DIM), lambda i: (0, 0))]
    args = [x, modtab, modtab, w, gq, gk]
    if rope:
        in_specs += [pl.BlockSpec((tm, HEAD_DIM), lambda i: (i % tiles_per_seq, 0))] * 2
        args += [cos, sin]
    body = functools.partial(_inproj_body, n_q=n_q, n_k=n_k, plain_col0=plain_col0, nat_col0=nat_col0,
                             n_nat_q=n_nat_q, rope=rope, scale=scale)
    return pl.pallas_call(
        body,
        out_shape=jax.ShapeDtypeStruct((r, width), BF16),
        grid=(r // tm,),
        in_specs=in_specs,
        out_specs=pl.BlockSpec((tm, width), lambda i: (i, 0)),
        compiler_params=_cparams("parallel"),
        name="inproj",
    )(*args)


def _softmax_pv_t(s, vs):
    m = jnp.max(s[0], axis=0, keepdims=True)
    for si in s[1:]:
        m = jnp.maximum(m, jnp.max(si, axis=0, keepdims=True))
    p = [jnp.exp2(si - m) for si in s]
    l = jnp.sum(p[0], axis=0, keepdims=True)
    for pi in p[1:]:
        l = l + jnp.sum(pi, axis=0, keepdims=True)
    ot = None
    for v, pi in zip(vs, p):
        part = lax.dot_general(v, pi.astype(BF16), (((0,), (0,)), ((), ())), preferred_element_type=F32)
        ot = part if ot is None else ot + part
    return ot * (1.0 / l)


def _attn_body(*refs, g, n_src, n_sub):
    q_ref = refs[0]
    k_refs = refs[1:1 + n_src]
    v_refs = refs[1 + n_src:1 + 2 * n_src]
    o_ref = refs[-1]
    tq = q_ref.shape[0]
    hs = g // n_sub
    scores = []
    for j in range(n_sub):
        heads = range(j * hs, (j + 1) * hs)
        qs = jnp.concatenate([q_ref[:, h * HEAD_DIM:(h + 1) * HEAD_DIM] for h in heads], axis=0)
        scores.append([lax.dot_general(k_ref[...], qs, (((1,), (1,)), ((), ())), preferred_element_type=F32)
                       for k_ref in k_refs])
    for j in range(n_sub):
        ot = _softmax_pv_t(scores[j], [v_ref[...] for v_ref in v_refs])
        for i in range(hs):
            h = j * hs + i
            o_ref[:, h * HEAD_DIM:(h + 1) * HEAD_DIM] = ot[:, i * tq:(i + 1) * tq].T.astype(o_ref.dtype)


def _attention(qarr, q_col0, n_q_heads, n_kv_heads, srcs, n_seq, seq_q, *, tq_target=256):
    g = n_q_heads // n_kv_heads
    gw = g * HEAD_DIM
    assert q_col0 % gw == 0
    tq = _tile(seq_q, tq_target, 128)
    tiles = seq_q // tq
    q_cb = q_col0 // gw
    in_specs = [pl.BlockSpec((tq, gw), lambda b, h, t: (b * tiles + t, q_cb + h))]
    args = [qarr]
    for which in (1, 2):
        for src in srcs:
            arr, col0, nk = src[0], src[which], src[3]
            cb = col0 // HEAD_DIM
            in_specs.append(pl.BlockSpec((nk, HEAD_DIM), lambda b, h, t, cb=cb: (b, cb + h)))
            args.append(arr)
    return pl.pallas_call(
        functools.partial(_attn_body, g=g, n_src=len(srcs), n_sub=g),
        out_shape=jax.ShapeDtypeStruct((n_seq * seq_q, n_q_heads * HEAD_DIM), F32),
        grid=(n_seq, n_kv_heads, tiles),
        in_specs=in_specs,
        out_specs=pl.BlockSpec((tq, gw), lambda b, h, t: (b * tiles + t, h)),
        compiler_params=_cparams("parallel", "parallel", "parallel"),
        name="attention",
    )(*args)


NAT_GROUP_ROWS = 4


def _nat_plan(rows, kr):
    ub_rows = min(kr + NAT_GROUP_ROWS, rows)
    assert rows % NAT_GROUP_ROWS == 0
    starts, pids, patterns = [], [], []
    for r0 in range(0, rows, NAT_GROUP_ROWS):
        band = lambda r: min(max(r - kr // 2, 0), rows - kr)
        ub = min(band(r0), rows - ub_rows)
        pat = tuple((band(r) - ub, r - ub) for r in range(r0, r0 + NAT_GROUP_ROWS))
        if pat not in patterns:
            patterns.append(pat)
        starts.append(ub)
        pids.append(patterns.index(pat))
    return ub_rows, starts, pids, patterns


def _nat_bias_body(rpb_ref, o_ref, *, win_r, win_c, kr, ub_rows, patterns):
    h = pl.program_id(0)
    w = GRID_W
    ck = lax.broadcasted_iota(jnp.int32, (w, w), 0)
    cq = lax.broadcasted_iota(jnp.int32, (w, w), 1)
    c_start = jnp.clip(cq - win_c // 2, 0, w - win_c)
    inside = (ck >= c_start) & (ck < c_start + win_c)
    d_col = ck - cq + (win_c - 1)
    n_dr = 2 * win_r - 1
    n_dc = 2 * win_c - 1
    tiles = []
    for dr in range(n_dr):
        t = jnp.zeros((w, w), F32)
        for dc in range(n_dc):
            t = jnp.where(d_col == dc, rpb_ref[(h * n_dr + dr) * n_dc + dc], t)
        tiles.append(jnp.where(inside, t * LOG2E, NEG_INF))
    masked = jnp.full((w, w), NEG_INF, F32)
    for p, pat in enumerate(patterns):
        for rq, (s_rel, r_rel) in enumerate(pat):
            for kk in range(ub_rows):
                blk = tiles[kk - r_rel + win_r - 1] if s_rel <= kk < s_rel + kr else masked
                o_ref[0, p, kk * w:(kk + 1) * w, rq * w:(rq + 1) * w] = blk


def _nat_bias(rpb, kr, ub_rows, patterns):
    nh, n_dr, n_dc = rpb.shape
    win_r, win_c = (n_dr + 1) // 2, (n_dc + 1) // 2
    shape = (len(patterns), ub_rows * GRID_W, NAT_GROUP_ROWS * GRID_W)
    return pl.pallas_call(
        functools.partial(_nat_bias_body, win_r=win_r, win_c=win_c, kr=kr, ub_rows=ub_rows, patterns=patterns),
        out_shape=jax.ShapeDtypeStruct((nh,) + shape, F32),
        grid=(nh,),
        in_specs=[pl.BlockSpec(memory_space=pltpu.SMEM)],
        out_specs=pl.BlockSpec((1,) + shape, lambda h: (h, 0, 0, 0)),
        compiler_params=_cparams("parallel"),
        name="nat_bias",
    )(rpb.reshape(-1))


def _nat_body(q_ref, k_ref, v_ref, kc_ref, vc_ref, bias_ref, o_ref, *, starts, pids):
    w = GRID_W
    gq = NAT_GROUP_ROWS * w
    nk = bias_ref.shape[2]
    nt = (((1,), (1,)), ((), ()))
    scores = []
    for g, (ub, pid) in enumerate(zip(starts, pids)):
        q = q_ref[g * gq:(g + 1) * gq, :]
        s_loc = lax.dot_general(k_ref[ub * w:ub * w + nk, :], q, nt, preferred_element_type=F32)
        s_ctx = lax.dot_general(kc_ref[...], q, nt, preferred_element_type=F32)
        scores.append([s_loc + bias_ref[0, pid], s_ctx])
    for g, ub in enumerate(starts):
        ot = _softmax_pv_t(scores[g], [v_ref[ub * w:ub * w + nk, :], vc_ref[...]])
        o_ref[g * gq:(g + 1) * gq, :] = ot.T.astype(o_ref.dtype)


def _natten(proj, cproj, col0, bias, starts, pids, n_seq, seq, n_ctx, n_heads):
    hb0 = col0 // HEAD_DIM
    blk = lambda part: pl.BlockSpec((seq, HEAD_DIM), lambda h, b: (b, hb0 + part * n_heads + h))
    cblk = lambda part: pl.BlockSpec((n_ctx, HEAD_DIM), lambda h, b: (b, hb0 + part * n_heads + h))
    return pl.pallas_call(
        functools.partial(_nat_body, starts=tuple(starts), pids=tuple(pids)),
        out_shape=jax.ShapeDtypeStruct((n_seq * seq, n_heads * HEAD_DIM), F32),
        grid=(n_heads, n_seq),
        in_specs=[blk(0), blk(1), blk(2), cblk(1), cblk(2),
                  pl.BlockSpec((1,) + bias.shape[1:], lambda h, b: (h, 0, 0, 0))],
        out_specs=pl.BlockSpec((seq, HEAD_DIM), lambda h, b: (b, h)),
        compiler_params=_cparams("parallel", "parallel"),
        name="natten",
    )(proj, proj, proj, cproj, cproj, bias)


LANES = 128


@functools.lru_cache(maxsize=None)
def _dft_tables(n):
    h = n // 2
    k = np.arange(h, dtype=np.int64)
    even = ((k[:, None] * (2 * k[None, :])) % (2 * n)).astype(np.float64) * (math.pi / n)
    odd = ((k[:, None] * (2 * k[None, :] + 1)) % (2 * n)).astype(np.float64) * (math.pi / n)
    to_bf16 = lambda a: np.asarray(a, np.float32).astype(BF16)
    return tuple(to_bf16(a) for a in (np.cos(even), np.sin(even), np.cos(odd), np.sin(odd),
                                      np.cos(odd).T, np.sin(odd).T))


@functools.lru_cache(maxsize=None)
def _filter_consts(n, n_bands, hy):
    t = np.linspace(0.0, 1.0, n)
    bands = np.arange(1, n_bands + 1, dtype=np.float64)
    ang = 2.0 * math.pi * t[:, None] * bands[None, :]
    feat = np.concatenate([t[:, None], np.cos(ang), np.sin(ang)], axis=-1)
    max_decay = math.log(HY_DECAY_TARGET) / HY_DECAY_SHORT_PCT
    min_decay = math.log(HY_DECAY_TARGET) / HY_DECAY_LONG_PCT
    deltas = np.abs(np.linspace(min_decay, max_decay, hy))
    window = np.exp(-t[:, None] * deltas[None, :]) + HY_DECAY_SHIFT
    return feat.astype(np.float32), window.astype(np.float32)


def _to_slabs(slab_ref, x):
    for s in range(slab_ref.shape[0]):
        slab_ref[s] = x[:, s * LANES:(s + 1) * LANES]


def _even_odd(slab_ref):
    ns, n, _ = slab_ref.shape
    cat = lambda parts: parts[0] if ns == 1 else jnp.concatenate(parts, axis=1)
    ev = cat([slab_ref[s, pl.ds(0, n // 2, stride=2), :] for s in range(ns)])
    od = cat([slab_ref[s, pl.ds(1, n // 2, stride=2), :] for s in range(ns)])
    return ev, od


def _alt_sign(h):
    j = lax.broadcasted_iota(jnp.int32, (h, 1), 0)
    return jnp.where((j & 1) == 0, 1.0, -1.0)


def _dot(a, b):
    return jnp.dot(a, b, preferred_element_type=F32)


def _filter_body(feat_ref, w1_ref, b1_ref, fr_ref, w2_ref, b2_ref, w3_ref, win_ref,
                 ce_ref, se_ref, co_ref, so_ref, spec_ref, mid_ref, hhi_sc, hlo_sc, ge_sc, go_sc, *, n, hy):
    j = pl.program_id(1)
    hp = lax.Precision.HIGHEST
    half = n // 2
    cw = win_ref.shape[1]

    @pl.when(j == 0)
    def _():
        fr = fr_ref[0]
        hid = jnp.sin(fr * (jnp.dot(feat_ref[...], w1_ref[0], precision=hp, preferred_element_type=F32) + b1_ref[0]))
        hid = jnp.sin(fr * (jnp.dot(hid, w2_ref[0], precision=hp, preferred_element_type=F32) + b2_ref[0]))
        hhi_sc[...] = hid.astype(BF16)
        hlo_sc[...] = (hid - hid.astype(BF16).astype(F32)).astype(BF16)

    def last_layer(w):
        w_hi = w.astype(BF16)
        w_lo = (w - w_hi.astype(F32)).astype(BF16)
        return _dot(hhi_sc[...], w_hi) + (_dot(hhi_sc[...], w_lo) + _dot(hlo_sc[...], w_hi))

    win = win_ref[...]
    t_idx = lax.broadcasted_iota(jnp.int32, (n, cw), 0)
    alt = _alt_sign(half)
    k_idx = lax.broadcasted_iota(jnp.int32, (half, cw), 0)
    wk = jnp.where(k_idx == 0, 1.0 / (2 * n), 2.0 / (2 * n))
    for o in range(HY_ORDER):
        fwd = last_layer(w3_ref[0, 2 * o]) * win
        bwd = last_layer(w3_ref[0, 2 * o + 1]) * win
        bwd = jnp.where(t_idx == 0, 0.0, bwd)
        inv = 1.0 / (jnp.sum(jnp.abs(fwd), axis=0, keepdims=True)
                     + jnp.sum(jnp.abs(bwd), axis=0, keepdims=True) + EPS)
        _to_slabs(ge_sc, (fwd + bwd) * inv)
        _to_slabs(go_sc, (fwd - bwd) * inv)
        ge_e, ge_o = _even_odd(ge_sc)
        go_e, go_o = _even_odd(go_sc)
        ac = _dot(ce_ref[...], ge_e.astype(BF16))
        bc = _dot(co_ref[...], ge_o.astype(BF16))
        a_s = _dot(se_ref[...], go_e.astype(BF16))
        bs = _dot(so_ref[...], go_o.astype(BF16))
        spec_ref[0, o, 0] = (ac + bc) * wk
        spec_ref[0, o, 1] = (a_s + bs) * wk
        spec_ref[0, o, 2] = (ac - bc) * wk
        spec_ref[0, o, 3] = (bs - a_s) * wk
        mid_r = jnp.sum(ge_e * alt, axis=0, keepdims=True) * (2.0 / (2 * n))
        mid_s = jnp.sum(go_o * alt, axis=0, keepdims=True) * (2.0 / (2 * n))
        row = lax.broadcasted_iota(jnp.int32, (8, cw), 0)
        mid_ref[0, o] = jnp.where(row == 0, mid_r, jnp.where(row == 1, mid_s, 0.0))


def _filter_spectrum(n, hf_w1, hf_b1, hf_freq, hf_w2, hf_b2, hf_w3, tabs, layers):
    nl = len(layers)
    n_pos, n_ffn = hf_w1.shape[1:]
    hy = hf_w3.shape[2] // (2 * HY_ORDER)
    half = n // 2
    cw = LANES
    feat, window = _filter_consts(n, (n_pos - 1) // 2, hy)
    lsel = jnp.asarray(layers, jnp.int32)
    lanes = lambda k: -(-k // LANES) * LANES
    kp, fp = lanes(n_pos), lanes(n_ffn)
    feat = np.pad(feat, ((0, 0), (0, kp - n_pos)))
    pick = lambda a, r, c: jnp.pad(a[lsel], ((0, 0), (0, r - a.shape[1]), (0, c - a.shape[2])))
    vec = lambda a: jnp.pad(a[lsel], ((0, 0), (0, fp - a.shape[1]))).reshape(nl, 1, fp)
    w3 = pick(hf_w3, fp, hf_w3.shape[2]).reshape(nl, fp, 2 * HY_ORDER, hy).transpose(0, 2, 1, 3)
    full = lambda shp: pl.BlockSpec((1,) + shp, lambda l, j: (l, 0, 0))
    tab = pl.BlockSpec((half, half), lambda l, j: (0, 0))
    return pl.pallas_call(
        functools.partial(_filter_body, n=n, hy=hy),
        out_shape=(jax.ShapeDtypeStruct((nl, HY_ORDER, 4, half, hy), F32),
                   jax.ShapeDtypeStruct((nl, HY_ORDER, 8, hy), F32)),
        grid=(nl, hy // cw),
        in_specs=[pl.BlockSpec((n, kp), lambda l, j: (0, 0)),
                  full((kp, fp)), full((1, fp)), full((1, fp)), full((fp, fp)), full((1, fp)),
                  pl.BlockSpec((1, 2 * HY_ORDER, fp, cw), lambda l, j: (l, 0, 0, j)),
                  pl.BlockSpec((n, cw), lambda l, j: (0, j)),
                  tab, tab, tab, tab],
        out_specs=(pl.BlockSpec((1, HY_ORDER, 4, half, cw), lambda l, j: (l, 0, 0, 0, j)),
                   pl.BlockSpec((1, HY_ORDER, 8, cw), lambda l, j: (l, 0, 0, j))),
        scratch_shapes=[pltpu.VMEM((n, fp), BF16), pltpu.VMEM((n, fp), BF16),
                        pltpu.VMEM((cw // LANES, n, LANES), F32), pltpu.VMEM((cw // LANES, n, LANES), F32)],
        compiler_params=_cparams("parallel", "arbitrary"),
        name="hyena_filter",
    )(feat, pick(hf_w1, kp, fp), vec(hf_b1), vec(hf_freq), pick(hf_w2, fp, fp), vec(hf_b2), w3, window,
      *tabs[:4])


def _conv3(x, w_ref, b_ref):
    n = x.shape[0]
    t = lax.broadcasted_iota(jnp.int32, x.shape, 0)
    prev = jnp.where(t == 0, 0.0, pltpu.roll(x, 1, 0))
    nxt = jnp.where(t == n - 1, 0.0, pltpu.roll(x, n - 1, 0))
    return b_ref[...] + w_ref[0:1, :] * prev + w_ref[1:2, :] * x + w_ref[2:3, :] * nxt


def _hyena_body(*refs, conv_z):
    ce_ref, se_ref, co_ref, so_ref, cot_ref, sot_ref, z_ref, gate_ref = refs[:8]
    i = 8
    if conv_z:
        wz_ref, bz_ref = refs[i:i + 2]
        i += 2
    wg_ref, bg_ref, skip_ref, spec_ref, mid_ref, o_ref, z_sc, y_sc = refs[i:]
    half = ce_ref.shape[0]
    z = z_ref[...].astype(F32)
    if conv_z:
        z = _conv3(z, wz_ref, bz_ref)
    _to_slabs(z_sc, z)
    ze, zo = _even_odd(z_sc)
    ze16, zo16 = ze.astype(BF16), zo.astype(BF16)
    ac, bc = _dot(ce_ref[...], ze16), _dot(co_ref[...], zo16)
    a_s, bs = _dot(se_ref[...], ze16), _dot(so_ref[...], zo16)
    alt = _alt_sign(half)

    def cmul(zr, zs, hr, hs):
        return zr * hr - zs * hs, zr * hs + zs * hr

    yr_lo, ys_lo = cmul(ac + bc, a_s + bs, spec_ref[0, 0, 0], spec_ref[0, 0, 1])
    yr_hi, ys_hi = cmul(ac - bc, bs - a_s, spec_ref[0, 0, 2], spec_ref[0, 0, 3])
    yr_mid, ys_mid = cmul(jnp.sum(ze * alt, axis=0, keepdims=True), jnp.sum(zo * alt, axis=0, keepdims=True),
                          mid_ref[0, 0, 0:1, :], mid_ref[0, 0, 1:2, :])
    y_e = (_dot(ce_ref[...], (yr_lo + yr_hi).astype(BF16)) + _dot(se_ref[...], (ys_lo - ys_hi).astype(BF16))
           + alt * yr_mid)
    y_o = (_dot(cot_ref[...], (yr_lo - yr_hi).astype(BF16)) + _dot(sot_ref[...], (ys_lo + ys_hi).astype(BF16))
           + alt * ys_mid)
    skip = skip_ref[0]
    y_e = y_e + skip * ze
    y_o = y_o + skip * zo
    for s in range(y_sc.shape[0]):
        y_sc[s, pl.ds(0, half, stride=2), :] = y_e[:, s * LANES:(s + 1) * LANES]
        y_sc[s, pl.ds(1, half, stride=2), :] = y_o[:, s * LANES:(s + 1) * LANES]
    gate = _conv3(gate_ref[...].astype(F32), wg_ref, bg_ref)
    for s in range(y_sc.shape[0]):
        o_ref[:, s * LANES:(s + 1) * LANES] = gate[:, s * LANES:(s + 1) * LANES] * y_sc[s]


def _hyena_order(z_arr, z_cb, hyp, hyp_col0, gate_part, short_w, short_b, spec, mid, li, order, skip, tabs,
                 n_seq, seq, hy, *, conv_z):
    half = seq // 2
    cw = min(hy, 2 * LANES)
    cpb = hy // cw
    assert hyp_col0 % cw == 0
    hcb = hyp_col0 // cw
    tab = pl.BlockSpec((half, half), lambda b, j: (0, 0))
    part = lambda p, rows: pl.BlockSpec((rows, cw), lambda b, j: (0, p * cpb + j))
    in_specs = [tab] * 6 + [pl.BlockSpec((seq, cw), lambda b, j: (b, z_cb + j)),
                            pl.BlockSpec((seq, cw), lambda b, j: (b, hcb + gate_part * cpb + j))]
    args = list(tabs) + [z_arr, hyp]
    if conv_z:
        in_specs += [part(0, short_w.shape[0]), part(0, 1)]
        args += [short_w, short_b.reshape(1, -1)]
    in_specs += [part(gate_part, short_w.shape[0]), part(gate_part, 1),
                 pl.BlockSpec((1, 1, cw), lambda b, j: (order, 0, j)),
                 pl.BlockSpec((1, 1, 4, half, cw), lambda b, j: (li, order, 0, 0, j)),
                 pl.BlockSpec((1, 1, 8, cw), lambda b, j: (li, order, 0, j))]
    args += [short_w, short_b.reshape(1, -1), skip.reshape(HY_ORDER, 1, hy), spec, mid]
    return pl.pallas_call(
        functools.partial(_hyena_body, conv_z=conv_z),
        out_shape=jax.ShapeDtypeStruct((n_seq * seq, hy), F32),
        grid=(n_seq, cpb),
        in_specs=in_specs,
        out_specs=pl.BlockSpec((seq, cw), lambda b, j: (b, j)),
        scratch_shapes=[pltpu.VMEM((cw // LANES, seq, LANES), F32), pltpu.VMEM((cw // LANES, seq, LANES), F32)],
        compiler_params=_cparams("parallel", "parallel"),
        name="hyena_order",
    )(*args)


def _hyena(proj, col0, hy, short_w, short_b, spec, mid, li, skip, tabs, n_seq, seq):
    kw = dict(short_w=short_w, short_b=short_b, spec=spec, mid=mid, li=li, skip=skip, tabs=tabs,
              n_seq=n_seq, seq=seq, hy=hy)
    cw = min(hy, 2 * LANES)
    z = _hyena_order(proj, col0 // cw, proj, col0, 1, order=0, conv_z=True, **kw)
    return _hyena_order(z, 0, proj, col0, 2, order=1, conv_z=False, **kw)


def _layer_norm(y, g, b):
    mu = jnp.mean(y, axis=-1, keepdims=True)
    yc = y - mu
    var = jnp.mean(yc * yc, axis=-1, keepdims=True)
    return yc * lax.rsqrt(var + EPS) * g + b


def _merge_body(oa_ref, ob_ref, oc_ref, x_ref, gt_ref, g_ref, w_ref, lg_ref, lb_ref, o_ref, mix_sc, *, alpha):
    col = 0
    for ref in (oa_ref, ob_ref, oc_ref):
        v = ref[...]
        wd = v.shape[1]
        vn = v * lax.rsqrt(jnp.mean(v * v, axis=-1, keepdims=True) + EPS)
        mix_sc[:, col:col + wd] = (vn * g_ref[:, col:col + wd]).astype(BF16)
        col += wd
    mix = jnp.dot(mix_sc[...], w_ref[...], preferred_element_type=F32)
    y = alpha * x_ref[...] + gt_ref[0, 0] * mix
    o_ref[...] = _layer_norm(y, lg_ref[...], lb_ref[...])


def _merge(oa, ob, oc, x, modtab, layer, seq, mod_row_of_tile, g_mix, w_out, ln_g, ln_b, alpha, *, tm_target=256):
    r, d = x.shape
    tm = _tile(seq, tm_target)
    tiles_per_seq = seq // tm
    mod_idx = lambda i: mod_row_of_tile(i // tiles_per_seq)
    row = lambda a: pl.BlockSpec((tm, a.shape[1]), lambda i: (i, 0))
    vec = pl.BlockSpec((1, d), lambda i: (0, 0))
    return pl.pallas_call(
        functools.partial(_merge_body, alpha=alpha),
        out_shape=jax.ShapeDtypeStruct((r, d), F32),
        grid=(r // tm,),
        in_specs=[row(oa), row(ob), row(oc), row(x),
                  pl.BlockSpec((1, 1, 1, d), lambda i: (layer, mod_idx(i), 0, 2)),
                  vec, pl.BlockSpec(w_out.shape, lambda i: (0, 0)), vec, vec],
        out_specs=pl.BlockSpec((tm, d), lambda i: (i, 0)),
        scratch_shapes=[pltpu.VMEM((tm, w_out.shape[0]), BF16)],
        compiler_params=_cparams("parallel"),
        name="merge_outproj_ln",
    )(oa, ob, oc, x, modtab, g_mix.reshape(1, -1), w_out, ln_g.reshape(1, -1), ln_b.reshape(1, -1))


def _mlp_body(x_ref, sh_ref, sc_ref, gt_ref, w1_ref, w2_ref, lg_ref, lb_ref, o_ref, u_sc, acc_sc, *, alpha):
    f = pl.program_id(1)

    @pl.when(f == 0)
    def _():
        u_sc[...] = (x_ref[...] * (1.0 + sc_ref[0, 0]) + sh_ref[0, 0]).astype(BF16)
        acc_sc[...] = jnp.zeros_like(acc_sc)

    h = jnp.maximum(jnp.dot(u_sc[...], w1_ref[...], preferred_element_type=F32), 0.0)
    acc_sc[...] += jnp.dot((h * h).astype(BF16), w2_ref[...], preferred_element_type=F32)

    @pl.when(f == pl.num_programs(1) - 1)
    def _():
        y = alpha * x_ref[...] + gt_ref[0, 0] * acc_sc[...]
        o_ref[...] = _layer_norm(y, lg_ref[...], lb_ref[...])


def _mlp(x, modtab, layer, seq, mod_row_of_tile, w1, w2, ln_g, ln_b, alpha, *, tm_target=512, tf_target=1024):
    r, d = x.shape
    dff = w1.shape[1]
    tm = _tile(seq, tm_target)
    tf = _tile(dff, tf_target, 128)
    tiles_per_seq = seq // tm
    mod_idx = lambda i: mod_row_of_tile(i // tiles_per_seq)
    mod = lambda k: pl.BlockSpec((1, 1, 1, d), lambda i, f: (layer, mod_idx(i), 0, k))
    vec = pl.BlockSpec((1, d), lambda i, f: (0, 0))
    return pl.pallas_call(
        functools.partial(_mlp_body, alpha=alpha),
        out_shape=jax.ShapeDtypeStruct((r, d), F32),
        grid=(r // tm, dff // tf),
        in_specs=[pl.BlockSpec((tm, d), lambda i, f: (i, 0)), mod(3), mod(4), mod(5),
                  pl.BlockSpec((d, tf), lambda i, f: (0, f)),
                  pl.BlockSpec((tf, d), lambda i, f: (f, 0)), vec, vec],
        out_specs=pl.BlockSpec((tm, d), lambda i, f: (i, 0)),
        scratch_shapes=[pltpu.VMEM((tm, d), BF16), pltpu.VMEM((tm, d), F32)],
        compiler_params=_cparams("parallel", "arbitrary"),
        name="mlp_ln",
    )(x, modtab, modtab, modtab, w1, w2, ln_g.reshape(1, -1), ln_b.reshape(1, -1))


@functools.lru_cache(maxsize=None)
def _rope_tables(n):
    t = np.arange(n)
    row = (t // GRID_W).astype(np.float64)
    col = (t % GRID_W).astype(np.float64)
    n_pairs_axis = HEAD_DIM // 4
    inv_freq = ROPE_THETA ** (-np.arange(n_pairs_axis, dtype=np.float64) / n_pairs_axis)
    ang = np.concatenate([row[:, None] * inv_freq[None, :], col[:, None] * inv_freq[None, :]], axis=-1)
    cos = np.repeat(np.cos(ang), 2, axis=-1)
    sin = np.stack([-np.sin(ang), np.sin(ang)], axis=-1).reshape(n, HEAD_DIM)
    return cos.astype(np.float32), sin.astype(np.float32)


def kernel(x, c, ctx, c_ctx, w_mod, b_mod, w_in, q_norm_g, k_norm_g, hy_short_w, hy_short_b, hf_w1, hf_b1,
           hf_freq, hf_w2, hf_b2, hf_w3, hy_bias, nat_rpb, g_mix, w_out, ln1_g, ln1_b, w1, w2, ln2_g, ln2_b):
    bsz, n_lat, d = x.shape
    n_ctx = ctx.shape[1]
    depth = w_mod.shape[0]
    n_mix_heads = d // HEAD_DIM
    a_heads = n_mix_heads // 2
    hy = (n_mix_heads // 4) * HEAD_DIM
    c_heads = n_mix_heads - a_heads - hy // HEAD_DIM
    a_w, kv_w, c_w = a_heads * HEAD_DIM, A_KV_HEADS * HEAD_DIM, c_heads * HEAD_DIM
    attn_w = a_w + 2 * kv_w
    hy_col0 = attn_w
    nat_col0 = attn_w + 3 * hy
    assert w_in.shape[2] == nat_col0 + 3 * c_w and n_lat % GRID_W == 0
    alpha = (2 * depth) ** 0.25
    scale = HEAD_DIM ** -0.5 * LOG2E
    rows = n_lat // GRID_W
    kr = min((nat_rpb.shape[2] + 1) // 2, rows)
    ub_rows, ub_starts, ub_pids, nat_patterns = _nat_plan(rows, kr)

    n_rows = -(-(bsz + 1) // 8) * 8
    cc = jnp.zeros((n_rows, d), F32).at[:bsz].set(c).at[bsz].set(c_ctx)
    modtab = _modulation(cc, w_mod, b_mod).reshape(depth, n_rows, 1, 6 * d)
    lat_mod = lambda b: b
    ctx_mod = lambda b: bsz

    cos, sin = _rope_tables(n_lat)
    tabs_lat = _dft_tables(n_lat)
    spec_lat, mid_lat = _filter_spectrum(n_lat, hf_w1, hf_b1, hf_freq, hf_w2, hf_b2, hf_w3, tabs_lat,
                                         list(range(depth)))
    if depth > 1:
        tabs_ctx = _dft_tables(n_ctx)
        spec_ctx, mid_ctx = _filter_spectrum(n_ctx, hf_w1, hf_b1, hf_freq, hf_w2, hf_b2, hf_w3, tabs_ctx,
                                             list(range(depth - 1)))

    xl = x.reshape(bsz * n_lat, d)
    hc = ctx.reshape(bsz * n_ctx, d)
    for l in range(depth):
        keep_ctx = l < depth - 1
        w_in_l = _to_bf16(w_in, l)
        gq = q_norm_g[l].reshape(1, HEAD_DIM)
        gk = k_norm_g[l].reshape(1, HEAD_DIM)
        bias = _nat_bias(nat_rpb[l], kr, ub_rows, tuple(nat_patterns))

        proj_kw = dict(n_q=a_heads, n_k=A_KV_HEADS, plain_col0=hy_col0, nat_col0=nat_col0, n_nat_q=c_heads,
                       scale=scale)
        proj = _inproj(xl, modtab, l, n_lat, lat_mod, w_in_l, gq, gk, cos=cos, sin=sin, **proj_kw)
        cproj = _inproj(hc, modtab, l, n_ctx, ctx_mod, w_in_l, gq, gk, **proj_kw)

        o_a = _attention(proj, 0, a_heads, A_KV_HEADS,
                         [(proj, a_w, a_w + kv_w, n_lat), (cproj, a_w, a_w + kv_w, n_ctx)], bsz, n_lat)
        o_b = _hyena(proj, hy_col0, hy, hy_short_w[l], hy_short_b[l], spec_lat, mid_lat, l, hy_bias[l], tabs_lat,
                     bsz, n_lat)
        o_c = _natten(proj, cproj, nat_col0, bias, ub_starts, ub_pids, bsz, n_lat, n_ctx, c_heads)
        w_out_l = _to_bf16(w_out, l)
        xl_new = _merge(o_a, o_b, o_c, xl, modtab, l, n_lat, lat_mod, g_mix[l], w_out_l, ln1_g[l], ln1_b[l], alpha)

        w1_l = _to_bf16(w1, l)
        w2_l = _to_bf16(w2, l)
        if keep_ctx:
            co_a = _attention(cproj, 0, a_heads, A_KV_HEADS, [(cproj, a_w, a_w + kv_w, n_ctx)], bsz, n_ctx)
            co_b = _hyena(cproj, hy_col0, hy, hy_short_w[l], hy_short_b[l], spec_ctx, mid_ctx, l, hy_bias[l],
                          tabs_ctx, bsz, n_ctx)
            co_c = _attention(cproj, nat_col0, c_heads, c_heads,
                              [(cproj, nat_col0 + c_w, nat_col0 + 2 * c_w, n_ctx)], bsz, n_ctx)
            hc = _merge(co_a, co_b, co_c, hc, modtab, l, n_ctx, ctx_mod, g_mix[l], w_out_l, ln1_g[l], ln1_b[l], alpha)
            hc = _mlp(hc, modtab, l, bsz * n_ctx, ctx_mod, w1_l, w2_l, ln2_g[l], ln2_b[l], alpha)
        xl = _mlp(xl_new, modtab, l, n_lat, lat_mod, w1_l, w2_l, ln2_g[l], ln2_b[l], alpha)
    return xl.reshape(bsz, n_lat, d)
```

```python
import functools
import math

import jax
import jax.numpy as jnp
import numpy as np
from jax import lax
from jax.experimental import pallas as pl
from jax.experimental.pallas import tpu as pltpu

F32 = jnp.float32
BF16 = jnp.bfloat16

HEAD_DIM = 128
GRID_W = 64
A_KV_HEADS = 2
ROPE_THETA = 10000.0
HY_ORDER = 2
HY_DECAY_TARGET = 1e-2
HY_DECAY_SHORT_PCT = 0.3
HY_DECAY_LONG_PCT = 1.5
HY_DECAY_SHIFT = 0.05
EPS = 1e-6
NEG_INF = -1e30
LOG2E = math.log2(math.e)
VMEM_LIMIT_BYTES = 56 * 1024 * 1024


def _cparams(*sem):
    return pltpu.CompilerParams(dimension_semantics=sem, vmem_limit_bytes=VMEM_LIMIT_BYTES)


def _tile(n, target, mult=8):
    if n <= target:
        return n
    t = (target // mult) * mult
    while t >= mult:
        if n % t == 0:
            return t
        t -= mult
    return n


def _cast_body(w_ref, o_ref):
    o_ref[...] = w_ref[0].astype(o_ref.dtype)


def _to_bf16(w, layer):
    _, r, c = w.shape
    tr = _tile(r, max(8, (8 * 1024 * 1024) // (4 * c)), 16)
    return pl.pallas_call(
        _cast_body,
        out_shape=jax.ShapeDtypeStruct((r, c), BF16),
        grid=(r // tr,),
        in_specs=[pl.BlockSpec((1, tr, c), lambda i: (layer, i, 0))],
        out_specs=pl.BlockSpec((tr, c), lambda i: (i, 0)),
        compiler_params=_cparams("parallel"),
        name="weight_to_bf16",
    )(w)


def _mod_body(c_ref, w_ref, b_ref, o_ref):
    c = c_ref[...]
    s = (c / (1.0 + jnp.exp(-c))).astype(BF16)
    o_ref[0] = jnp.dot(s, w_ref[0].astype(BF16), preferred_element_type=F32) + b_ref[0]


def _modulation(cc, w_mod, b_mod):
    nl, d, n6 = w_mod.shape
    r = cc.shape[0]
    tn = _tile(n6, 1536, 128)
    return pl.pallas_call(
        _mod_body,
        out_shape=jax.ShapeDtypeStruct((nl, r, n6), F32),
        grid=(nl, n6 // tn),
        in_specs=[pl.BlockSpec((r, d), lambda l, j: (0, 0)),
                  pl.BlockSpec((1, d, tn), lambda l, j: (l, 0, j)),
                  pl.BlockSpec((1, 1, tn), lambda l, j: (l, 0, j))],
        out_specs=pl.BlockSpec((1, r, tn), lambda l, j: (l, 0, j)),
        compiler_params=_cparams("parallel", "parallel"),
        name="modulation",
    )(cc, w_mod, b_mod.reshape(nl, 1, n6))


def _rms_head(xh, g):
    return xh * lax.rsqrt(jnp.mean(xh * xh, axis=-1, keepdims=True) + EPS) * g


def _rope_head(y, cos, sin):
    lane = lax.broadcasted_iota(jnp.int32, y.shape, 1)
    swapped = jnp.where((lane & 1) == 0, pltpu.roll(y, HEAD_DIM - 1, 1), pltpu.roll(y, 1, 1))
    return y * cos + swapped * sin


def _inproj_body(*refs, n_q, n_k, plain_col0, nat_col0, n_nat_q, rope, scale):
    x_ref, sh_ref, sc_ref, w_ref, gq_ref, gk_ref = refs[:6]
    o_ref = refs[-1]
    width = w_ref.shape[1]
    u = (x_ref[...] * (1.0 + sc_ref[0, 0]) + sh_ref[0, 0]).astype(BF16)
    acc_attn = jnp.dot(u, w_ref[:, :plain_col0], preferred_element_type=F32)
    acc_plain = jnp.dot(u, w_ref[:, plain_col0:nat_col0], preferred_element_type=F32)
    acc_nat = jnp.dot(u, w_ref[:, nat_col0:], preferred_element_type=F32)
    gq = gq_ref[...] * scale
    gk = gk_ref[...]
    if rope:
        cos = refs[6][...]
        sin = refs[7][...]
    for hb in range(plain_col0 // HEAD_DIM):
        xh = acc_attn[:, hb * HEAD_DIM:(hb + 1) * HEAD_DIM]
        if hb < n_q + n_k:
            xh = _rms_head(xh, gq if hb < n_q else gk)
            if rope:
                xh = _rope_head(xh, cos, sin)
        o_ref[:, hb * HEAD_DIM:(hb + 1) * HEAD_DIM] = xh.astype(o_ref.dtype)
    o_ref[:, plain_col0:nat_col0] = acc_plain.astype(o_ref.dtype)
    for hb in range((width - nat_col0) // HEAD_DIM):
        xh = acc_nat[:, hb * HEAD_DIM:(hb + 1) * HEAD_DIM]
        if hb < n_nat_q:
            xh = xh * scale
        o_ref[:, nat_col0 + hb * HEAD_DIM:nat_col0 + (hb + 1) * HEAD_DIM] = xh.astype(o_ref.dtype)


def _inproj(x, modtab, layer, seq, mod_row_of_tile, w, gq, gk, *, n_q, n_k, plain_col0, nat_col0, n_nat_q,
            cos=None, sin=None, scale=1.0, tm_target=512):
    r, d = x.shape
    width = w.shape[1]
    tm = _tile(seq, tm_target)
    tiles_per_seq = seq // tm
    rope = cos is not None
    mod_idx = lambda i: mod_row_of_tile(i // tiles_per_seq)
    in_specs = [pl.BlockSpec((tm, d), lambda i: (i, 0)),
                pl.BlockSpec((1, 1, 1, d), lambda i: (layer, mod_idx(i), 0, 0)),
                pl.BlockSpec((1, 1, 1, d), lambda i: (layer, mod_idx(i), 0, 1)),
                pl.BlockSpec((d, width), lambda i: (0, 0)),
                pl.BlockSpec((1, HEAD_DIM), lambda i: (0, 0)), pl.BlockSpec((1, HEAD_DIM), lambda i: (0, 0))]
    args = [x, modtab, modtab, w, gq, gk]
    if rope:
        in_specs += [pl.BlockSpec((tm, HEAD_DIM), lambda i: (i % tiles_per_seq, 0))] * 2
        args += [cos, sin]
    body = functools.partial(_inproj_body, n_q=n_q, n_k=n_k, plain_col0=plain_col0, nat_col0=nat_col0,
                             n_nat_q=n_nat_q, rope=rope, scale=scale)
    return pl.pallas_call(
        body,
        out_shape=jax.ShapeDtypeStruct((r, width), BF16),
        grid=(r // tm,),
        in_specs=in_specs,
        out_specs=pl.BlockSpec((tm, width), lambda i: (i, 0)),
        compiler_params=_cparams("parallel"),
        name="inproj",
    )(*args)


def _softmax_pv_t(s, vs):
    m = jnp.max(s[0], axis=0, keepdims=True)
    for si in s[1:]:
        m = jnp.maximum(m, jnp.max(si, axis=0, keepdims=True))
    p = [jnp.exp2(si - m) for si in s]
    l = jnp.sum(p[0], axis=0, keepdims=True)
    for pi in p[1:]:
        l = l + jnp.sum(pi, axis=0, keepdims=True)
    ot = None
    for v, pi in zip(vs, p):
        part = lax.dot_general(v, pi.astype(BF16), (((0,), (0,)), ((), ())), preferred_element_type=F32)
        ot = part if ot is None else ot + part
    return ot * (1.0 / l)


def _attn_body(*refs, g, n_src, n_sub):
    q_ref = refs[0]
    k_refs = refs[1:1 + n_src]
    v_refs = refs[1 + n_src:1 + 2 * n_src]
    o_ref = refs[-1]
    tq = q_ref.shape[0]
    hs = g // n_sub
    scores = []
    for j in range(n_sub):
        heads = range(j * hs, (j + 1) * hs)
        qs = jnp.concatenate([q_ref[:, h * HEAD_DIM:(h + 1) * HEAD_DIM] for h in heads], axis=0)
        scores.append([lax.dot_general(k_ref[...], qs, (((1,), (1,)), ((), ())), preferred_element_type=F32)
                       for k_ref in k_refs])
    for j in range(n_sub):
        ot = _softmax_pv_t(scores[j], [v_ref[...] for v_ref in v_refs])
        for i in range(hs):
            h = j * hs + i
            o_ref[:, h * HEAD_DIM:(h + 1) * HEAD_DIM] = ot[:, i * tq:(i + 1) * tq].T.astype(o_ref.dtype)


def _attention(qarr, q_col0, n_q_heads, n_kv_heads, srcs, n_seq, seq_q, *, tq_target=512):
    g = n_q_heads // n_kv_heads
    gw = g * HEAD_DIM
    assert q_col0 % gw == 0
    tq = _tile(seq_q, tq_target, 128)
    tiles = seq_q // tq
    q_cb = q_col0 // gw
    in_specs = [pl.BlockSpec((tq, gw), lambda b, h, t: (b * tiles + t, q_cb + h))]
    args = [qarr]
    for which in (1, 2):
        for src in srcs:
            arr, col0, nk = src[0], src[which], src[3]
            cb = col0 // HEAD_DIM
            in_specs.append(pl.BlockSpec((nk, HEAD_DIM), lambda b, h, t, cb=cb: (b, cb + h)))
            args.append(arr)
    return pl.pallas_call(
        functools.partial(_attn_body, g=g, n_src=len(srcs), n_sub=g),
        out_shape=jax.ShapeDtypeStruct((n_seq * seq_q, n_q_heads * HEAD_DIM), F32),
        grid=(n_seq, n_kv_heads, tiles),
        in_specs=in_specs,
        out_specs=pl.BlockSpec((tq, gw), lambda b, h, t: (b * tiles + t, h)),
        compiler_params=_cparams("parallel", "parallel", "parallel"),
        name="attention",
    )(*args)


NAT_GROUP_ROWS = 4


def _nat_plan(rows, kr):
    ub_rows = min(kr + NAT_GROUP_ROWS, rows)
    assert rows % NAT_GROUP_ROWS == 0
    starts, pids, patterns = [], [], []
    for r0 in range(0, rows, NAT_GROUP_ROWS):
        band = lambda r: min(max(r - kr // 2, 0), rows - kr)
        ub = min(band(r0), rows - ub_rows)
        pat = tuple((band(r) - ub, r - ub) for r in range(r0, r0 + NAT_GROUP_ROWS))
        if pat not in patterns:
            patterns.append(pat)
        starts.append(ub)
        pids.append(patterns.index(pat))
    return ub_rows, starts, pids, patterns


def _nat_bias_body(rpb_ref, o_ref, *, win_r, win_c, kr, ub_rows, patterns):
    h = pl.program_id(0)
    w = GRID_W
    ck = lax.broadcasted_iota(jnp.int32, (w, w), 0)
    cq = lax.broadcasted_iota(jnp.int32, (w, w), 1)
    c_start = jnp.clip(cq - win_c // 2, 0, w - win_c)
    inside = (ck >= c_start) & (ck < c_start + win_c)
    d_col = ck - cq + (win_c - 1)
    n_dr = 2 * win_r - 1
    n_dc = 2 * win_c - 1
    tiles = []
    for dr in range(n_dr):
        t = jnp.zeros((w, w), F32)
        for dc in range(n_dc):
            t = jnp.where(d_col == dc, rpb_ref[(h * n_dr + dr) * n_dc + dc], t)
        tiles.append(jnp.where(inside, t * LOG2E, NEG_INF))
    masked = jnp.full((w, w), NEG_INF, F32)
    for p, pat in enumerate(patterns):
        for rq, (s_rel, r_rel) in enumerate(pat):
            for kk in range(ub_rows):
                blk = tiles[kk - r_rel + win_r - 1] if s_rel <= kk < s_rel + kr else masked
                o_ref[0, p, kk * w:(kk + 1) * w, rq * w:(rq + 1) * w] = blk


def _nat_bias(rpb, kr, ub_rows, patterns):
    nh, n_dr, n_dc = rpb.shape
    win_r, win_c = (n_dr + 1) // 2, (n_dc + 1) // 2
    shape = (len(patterns), ub_rows * GRID_W, NAT_GROUP_ROWS * GRID_W)
    return pl.pallas_call(
        functools.partial(_nat_bias_body, win_r=win_r, win_c=win_c, kr=kr, ub_rows=ub_rows, patterns=patterns),
        out_shape=jax.ShapeDtypeStruct((nh,) + shape, F32),
        grid=(nh,),
        in_specs=[pl.BlockSpec(memory_space=pltpu.SMEM)],
        out_specs=pl.BlockSpec((1,) + shape, lambda h: (h, 0, 0, 0)),
        compiler_params=_cparams("parallel"),
        name="nat_bias",
    )(rpb.reshape(-1))


def _nat_body(q_ref, k_ref, v_ref, kc_ref, vc_ref, bias_ref, o_ref, *, starts, pids):
    w = GRID_W
    gq = NAT_GROUP_ROWS * w
    nk = bias_ref.shape[2]
    nt = (((1,), (1,)), ((), ()))
    scores = []
    for g, (ub, pid) in enumerate(zip(starts, pids)):
        q = q_ref[g * gq:(g + 1) * gq, :]
        s_loc = lax.dot_general(k_ref[ub * w:ub * w + nk, :], q, nt, preferred_element_type=F32)
        s_ctx = lax.dot_general(kc_ref[...], q, nt, preferred_element_type=F32)
        scores.append([s_loc + bias_ref[0, pid], s_ctx])
    for g, ub in enumerate(starts):
        ot = _softmax_pv_t(scores[g], [v_ref[ub * w:ub * w + nk, :], vc_ref[...]])
        o_ref[g * gq:(g + 1) * gq, :] = ot.T.astype(o_ref.dtype)


def _natten(proj, cproj, col0, bias, starts, pids, n_seq, seq, n_ctx, n_heads):
    hb0 = col0 // HEAD_DIM
    blk = lambda part: pl.BlockSpec((seq, HEAD_DIM), lambda h, b: (b, hb0 + part * n_heads + h))
    cblk = lambda part: pl.BlockSpec((n_ctx, HEAD_DIM), lambda h, b: (b, hb0 + part * n_heads + h))
    return pl.pallas_call(
        functools.partial(_nat_body, starts=tuple(starts), pids=tuple(pids)),
        out_shape=jax.ShapeDtypeStruct((n_seq * seq, n_heads * HEAD_DIM), F32),
        grid=(n_heads, n_seq),
        in_specs=[blk(0), blk(1), blk(2), cblk(1), cblk(2),
                  pl.BlockSpec((1,) + bias.shape[1:], lambda h, b: (h, 0, 0, 0))],
        out_specs=pl.BlockSpec((seq, HEAD_DIM), lambda h, b: (b, h)),
        compiler_params=_cparams("parallel", "parallel"),
        name="natten",
    )(proj, proj, proj, cproj, cproj, bias)


LANES = 128


@functools.lru_cache(maxsize=None)
def _dft_tables(n):
    h = n // 2
    k = np.arange(h, dtype=np.int64)
    even = ((k[:, None] * (2 * k[None, :])) % (2 * n)).astype(np.float64) * (math.pi / n)
    odd = ((k[:, None] * (2 * k[None, :] + 1)) % (2 * n)).astype(np.float64) * (math.pi / n)
    to_bf16 = lambda a: np.asarray(a, np.float32).astype(BF16)
    return tuple(to_bf16(a) for a in (np.cos(even), np.sin(even), np.cos(odd), np.sin(odd),
                                      np.cos(odd).T, np.sin(odd).T))


@functools.lru_cache(maxsize=None)
def _filter_consts(n, n_bands, hy):
    t = np.linspace(0.0, 1.0, n)
    bands = np.arange(1, n_bands + 1, dtype=np.float64)
    ang = 2.0 * math.pi * t[:, None] * bands[None, :]
    feat = np.concatenate([t[:, None], np.cos(ang), np.sin(ang)], axis=-1)
    max_decay = math.log(HY_DECAY_TARGET) / HY_DECAY_SHORT_PCT
    min_decay = math.log(HY_DECAY_TARGET) / HY_DECAY_LONG_PCT
    deltas = np.abs(np.linspace(min_decay, max_decay, hy))
    window = np.exp(-t[:, None] * deltas[None, :]) + HY_DECAY_SHIFT
    return feat.astype(np.float32), window.astype(np.float32)


def _to_slabs(slab_ref, x):
    for s in range(slab_ref.shape[0]):
        slab_ref[s] = x[:, s * LANES:(s + 1) * LANES]


def _even_odd(slab_ref):
    ns, n, _ = slab_ref.shape
    cat = lambda parts: parts[0] if ns == 1 else jnp.concatenate(parts, axis=1)
    ev = cat([slab_ref[s, pl.ds(0, n // 2, stride=2), :] for s in range(ns)])
    od = cat([slab_ref[s, pl.ds(1, n // 2, stride=2), :] for s in range(ns)])
    return ev, od


def _alt_sign(h):
    j = lax.broadcasted_iota(jnp.int32, (h, 1), 0)
    return jnp.where((j & 1) == 0, 1.0, -1.0)


def _dot(a, b):
    return jnp.dot(a, b, preferred_element_type=F32)


def _filter_body(feat_ref, w1_ref, b1_ref, fr_ref, w2_ref, b2_ref, w3_ref, win_ref,
                 ce_ref, se_ref, co_ref, so_ref, spec_ref, mid_ref, hhi_sc, hlo_sc, ge_sc, go_sc, *, n, hy):
    j = pl.program_id(1)
    hp = lax.Precision.HIGHEST
    half = n // 2
    cw = win_ref.shape[1]

    @pl.when(j == 0)
    def _():
        fr = fr_ref[0]
        hid = jnp.sin(fr * (jnp.dot(feat_ref[...], w1_ref[0], precision=hp, preferred_element_type=F32) + b1_ref[0]))
        hid = jnp.sin(fr * (jnp.dot(hid, w2_ref[0], precision=hp, preferred_element_type=F32) + b2_ref[0]))
        hhi_sc[...] = hid.astype(BF16)
        hlo_sc[...] = (hid - hid.astype(BF16).astype(F32)).astype(BF16)

    def last_layer(w):
        w_hi = w.astype(BF16)
        w_lo = (w - w_hi.astype(F32)).astype(BF16)
        return _dot(hhi_sc[...], w_hi) + (_dot(hhi_sc[...], w_lo) + _dot(hlo_sc[...], w_hi))

    win = win_ref[...]
    t_idx = lax.broadcasted_iota(jnp.int32, (n, cw), 0)
    alt = _alt_sign(half)
    k_idx = lax.broadcasted_iota(jnp.int32, (half, cw), 0)
    wk = jnp.where(k_idx == 0, 1.0 / (2 * n), 2.0 / (2 * n))
    for o in range(HY_ORDER):
        fwd = last_layer(w3_ref[0, 2 * o]) * win
        bwd = last_layer(w3_ref[0, 2 * o + 1]) * win
        bwd = jnp.where(t_idx == 0, 0.0, bwd)
        inv = 1.0 / (jnp.sum(jnp.abs(fwd), axis=0, keepdims=True)
                     + jnp.sum(jnp.abs(bwd), axis=0, keepdims=True) + EPS)
        _to_slabs(ge_sc, (fwd + bwd) * inv)
        _to_slabs(go_sc, (fwd - bwd) * inv)
        ge_e, ge_o = _even_odd(ge_sc)
        go_e, go_o = _even_odd(go_sc)
        ac = _dot(ce_ref[...], ge_e.astype(BF16))
        bc = _dot(co_ref[...], ge_o.astype(BF16))
        a_s = _dot(se_ref[...], go_e.astype(BF16))
        bs = _dot(so_ref[...], go_o.astype(BF16))
        spec_ref[0, o, 0] = (ac + bc) * wk
        spec_ref[0, o, 1] = (a_s + bs) * wk
        spec_ref[0, o, 2] = (ac - bc) * wk
        spec_ref[0, o, 3] = (bs - a_s) * wk
        mid_r = jnp.sum(ge_e * alt, axis=0, keepdims=True) * (2.0 / (2 * n))
        mid_s = jnp.sum(go_o * alt, axis=0, keepdims=True) * (2.0 / (2 * n))
        row = lax.broadcasted_iota(jnp.int32, (8, cw), 0)
        mid_ref[0, o] = jnp.where(row == 0, mid_r, jnp.where(row == 1, mid_s, 0.0))


def _filter_spectrum(n, hf_w1, hf_b1, hf_freq, hf_w2, hf_b2, hf_w3, tabs, layers):
    nl = len(layers)
    n_pos, n_ffn = hf_w1.shape[1:]
    hy = hf_w3.shape[2] // (2 * HY_ORDER)
    half = n // 2
    cw = LANES
    feat, window = _filter_consts(n, (n_pos - 1) // 2, hy)
    lsel = jnp.asarray(layers, jnp.int32)
    lanes = lambda k: -(-k // LANES) * LANES
    kp, fp = lanes(n_pos), lanes(n_ffn)
    feat = np.pad(feat, ((0, 0), (0, kp - n_pos)))
    pick = lambda a, r, c: jnp.pad(a[lsel], ((0, 0), (0, r - a.shape[1]), (0, c - a.shape[2])))
    vec = lambda a: jnp.pad(a[lsel], ((0, 0), (0, fp - a.shape[1]))).reshape(nl, 1, fp)
    w3 = pick(hf_w3, fp, hf_w3.shape[2]).reshape(nl, fp, 2 * HY_ORDER, hy).transpose(0, 2, 1, 3)
    full = lambda shp: pl.BlockSpec((1,) + shp, lambda l, j: (l, 0, 0))
    tab = pl.BlockSpec((half, half), lambda l, j: (0, 0))
    return pl.pallas_call(
        functools.partial(_filter_body, n=n, hy=hy),
        out_shape=(jax.ShapeDtypeStruct((nl, HY_ORDER, 4, half, hy), F32),
                   jax.ShapeDtypeStruct((nl, HY_ORDER, 8, hy), F32)),
        grid=(nl, hy // cw),
        in_specs=[pl.BlockSpec((n, kp), lambda l, j: (0, 0)),
                  full((kp, fp)), full((1, fp)), full((1, fp)), full((fp, fp)), full((1, fp)),
                  pl.BlockSpec((1, 2 * HY_ORDER, fp, cw), lambda l, j: (l, 0, 0, j)),
                  pl.BlockSpec((n, cw), lambda l, j: (0, j)),
                  tab, tab, tab, tab],
        out_specs=(pl.BlockSpec((1, HY_ORDER, 4, half, cw), lambda l, j: (l, 0, 0, 0, j)),
                   pl.BlockSpec((1, HY_ORDER, 8, cw), lambda l, j: (l, 0, 0, j))),
        scratch_shapes=[pltpu.VMEM((n, fp), BF16), pltpu.VMEM((n, fp), BF16),
                        pltpu.VMEM((cw // LANES, n, LANES), F32), pltpu.VMEM((cw // LANES, n, LANES), F32)],
        compiler_params=_cparams("parallel", "arbitrary"),
        name="hyena_filter",
    )(feat, pick(hf_w1, kp, fp), vec(hf_b1), vec(hf_freq), pick(hf_w2, fp, fp), vec(hf_b2), w3, window,
      *tabs[:4])


def _conv3(x, w_ref, b_ref):
    n = x.shape[0]
    t = lax.broadcasted_iota(jnp.int32, x.shape, 0)
    prev = jnp.where(t == 0, 0.0, pltpu.roll(x, 1, 0))
    nxt = jnp.where(t == n - 1, 0.0, pltpu.roll(x, n - 1, 0))
    return b_ref[...] + w_ref[0:1, :] * prev + w_ref[1:2, :] * x + w_ref[2:3, :] * nxt


def _hyena_body(*refs, conv_z):
    ce_ref, se_ref, co_ref, so_ref, cot_ref, sot_ref, z_ref, gate_ref = refs[:8]
    i = 8
    if conv_z:
        wz_ref, bz_ref = refs[i:i + 2]
        i += 2
    wg_ref, bg_ref, skip_ref, spec_ref, mid_ref, o_ref, z_sc, y_sc = refs[i:]
    half = ce_ref.shape[0]
    z = z_ref[...].astype(F32)
    if conv_z:
        z = _conv3(z, wz_ref, bz_ref)
    _to_slabs(z_sc, z)
    ze, zo = _even_odd(z_sc)
    ze16, zo16 = ze.astype(BF16), zo.astype(BF16)
    ac, bc = _dot(ce_ref[...], ze16), _dot(co_ref[...], zo16)
    a_s, bs = _dot(se_ref[...], ze16), _dot(so_ref[...], zo16)
    alt = _alt_sign(half)

    def cmul(zr, zs, hr, hs):
        return zr * hr - zs * hs, zr * hs + zs * hr

    yr_lo, ys_lo = cmul(ac + bc, a_s + bs, spec_ref[0, 0, 0], spec_ref[0, 0, 1])
    yr_hi, ys_hi = cmul(ac - bc, bs - a_s, spec_ref[0, 0, 2], spec_ref[0, 0, 3])
    yr_mid, ys_mid = cmul(jnp.sum(ze * alt, axis=0, keepdims=True), jnp.sum(zo * alt, axis=0, keepdims=True),
                          mid_ref[0, 0, 0:1, :], mid_ref[0, 0, 1:2, :])
    y_e = (_dot(ce_ref[...], (yr_lo + yr_hi).astype(BF16)) + _dot(se_ref[...], (ys_lo - ys_hi).astype(BF16))
           + alt * yr_mid)
    y_o = (_dot(cot_ref[...], (yr_lo - yr_hi).astype(BF16)) + _dot(sot_ref[...], (ys_lo + ys_hi).astype(BF16))
           + alt * ys_mid)
    skip = skip_ref[0]
    y_e = y_e + skip * ze
    y_o = y_o + skip * zo
    for s in range(y_sc.shape[0]):
        y_sc[s, pl.ds(0, half, stride=2), :] = y_e[:, s * LANES:(s + 1) * LANES]
        y_sc[s, pl.ds(1, half, stride=2), :] = y_o[:, s * LANES:(s + 1) * LANES]
    gate = _conv3(gate_ref[...].astype(F32), wg_ref, bg_ref)
    for s in range(y_sc.shape[0]):
        o_ref[:, s * LANES:(s + 1) * LANES] = gate[:, s * LANES:(s + 1) * LANES] * y_sc[s]


def _hyena_order(z_arr, z_cb, hyp, hyp_col0, gate_part, short_w, short_b, spec, mid, li, order, skip, tabs,
                 n_seq, seq, hy, *, conv_z):
    half = seq // 2
    cw = min(hy, 2 * LANES)
    cpb = hy // cw
    assert hyp_col0 % cw == 0
    hcb = hyp_col0 // cw
    tab = pl.BlockSpec((half, half), lambda b, j: (0, 0))
    part = lambda p, rows: pl.BlockSpec((rows, cw), lambda b, j: (0, p * cpb + j))
    in_specs = [tab] * 6 + [pl.BlockSpec((seq, cw), lambda b, j: (b, z_cb + j)),
                            pl.BlockSpec((seq, cw), lambda b, j: (b, hcb + gate_part * cpb + j))]
    args = list(tabs) + [z_arr, hyp]
    if conv_z:
        in_specs += [part(0, short_w.shape[0]), part(0, 1)]
        args += [short_w, short_b.reshape(1, -1)]
    in_specs += [part(gate_part, short_w.shape[0]), part(gate_part, 1),
                 pl.BlockSpec((1, 1, cw), lambda b, j: (order, 0, j)),
                 pl.BlockSpec((1, 1, 4, half, cw), lambda b, j: (li, order, 0, 0, j)),
                 pl.BlockSpec((1, 1, 8, cw), lambda b, j: (li, order, 0, j))]
    args += [short_w, short_b.reshape(1, -1), skip.reshape(HY_ORDER, 1, hy), spec, mid]
    return pl.pallas_call(
        functools.partial(_hyena_body, conv_z=conv_z),
        out_shape=jax.ShapeDtypeStruct((n_seq * seq, hy), F32),
        grid=(n_seq, cpb),
        in_specs=in_specs,
        out_specs=pl.BlockSpec((seq, cw), lambda b, j: (b, j)),
        scratch_shapes=[pltpu.VMEM((cw // LANES, seq, LANES), F32), pltpu.VMEM((cw // LANES, seq, LANES), F32)],
        compiler_params=_cparams("parallel", "parallel"),
        name="hyena_order",
    )(*args)


def _hyena(proj, col0, hy, short_w, short_b, spec, mid, li, skip, tabs, n_seq, seq):
    kw = dict(short_w=short_w, short_b=short_b, spec=spec, mid=mid, li=li, skip=skip, tabs=tabs,
              n_seq=n_seq, seq=seq, hy=hy)
    cw = min(hy, 2 * LANES)
    z = _hyena_order(proj, col0 // cw, proj, col0, 1, order=0, conv_z=True, **kw)
    return _hyena_order(z, 0, proj, col0, 2, order=1, conv_z=False, **kw)


def _layer_norm(y, g, b):
    mu = jnp.mean(y, axis=-1, keepdims=True)
    yc = y - mu
    var = jnp.mean(yc * yc, axis=-1, keepdims=True)
    return yc * lax.rsqrt(var + EPS) * g + b


def _merge_body(oa_ref, ob_ref, oc_ref, x_ref, gt_ref, g_ref, w_ref, lg_ref, lb_ref, o_ref, mix_sc, *, alpha, n_sub):
    tm = x_ref.shape[0]
    rs = tm // n_sub
    mixes = []
    for j in range(n_sub):
        rows = slice(j * rs, (j + 1) * rs)
        col = 0
        for ref in (oa_ref, ob_ref, oc_ref):
            v = ref[rows, :]
            wd = v.shape[1]
            vn = v * lax.rsqrt(jnp.mean(v * v, axis=-1, keepdims=True) + EPS)
            mix_sc[rows, col:col + wd] = (vn * g_ref[:, col:col + wd]).astype(BF16)
            col += wd
        mixes.append(jnp.dot(mix_sc[rows, :], w_ref[...], preferred_element_type=F32))
    for j in range(n_sub):
        rows = slice(j * rs, (j + 1) * rs)
        y = alpha * x_ref[rows, :] + gt_ref[0, 0] * mixes[j]
        o_ref[rows, :] = _layer_norm(y, lg_ref[...], lb_ref[...])


def _merge(oa, ob, oc, x, modtab, layer, seq, mod_row_of_tile, g_mix, w_out, ln_g, ln_b, alpha, *, tm_target=512):
    r, d = x.shape
    tm = _tile(seq, tm_target)
    tiles_per_seq = seq // tm
    mod_idx = lambda i: mod_row_of_tile(i // tiles_per_seq)
    row = lambda a: pl.BlockSpec((tm, a.shape[1]), lambda i: (i, 0))
    vec = pl.BlockSpec((1, d), lambda i: (0, 0))
    return pl.pallas_call(
        functools.partial(_merge_body, alpha=alpha, n_sub=2 if tm % 32 == 0 else 1),
        out_shape=jax.ShapeDtypeStruct((r, d), F32),
        grid=(r // tm,),
        in_specs=[row(oa), row(ob), row(oc), row(x),
                  pl.BlockSpec((1, 1, 1, d), lambda i: (layer, mod_idx(i), 0, 2)),
                  vec, pl.BlockSpec(w_out.shape, lambda i: (0, 0)), vec, vec],
        out_specs=pl.BlockSpec((tm, d), lambda i: (i, 0)),
        scratch_shapes=[pltpu.VMEM((tm, w_out.shape[0]), BF16)],
        compiler_params=_cparams("parallel"),
        name="merge_outproj_ln",
    )(oa, ob, oc, x, modtab, g_mix.reshape(1, -1), w_out, ln_g.reshape(1, -1), ln_b.reshape(1, -1))


def _mlp_body(x_ref, sh_ref, sc_ref, gt_ref, w1_ref, w2_ref, lg_ref, lb_ref, o_ref, u_sc, acc_sc, *, alpha):
    f = pl.program_id(1)

    @pl.when(f == 0)
    def _():
        u_sc[...] = (x_ref[...] * (1.0 + sc_ref[0, 0]) + sh_ref[0, 0]).astype(BF16)
        acc_sc[...] = jnp.zeros_like(acc_sc)

    h = jnp.maximum(jnp.dot(u_sc[...], w1_ref[...], preferred_element_type=F32), 0.0)
    acc_sc[...] += jnp.dot((h * h).astype(BF16), w2_ref[...], preferred_element_type=F32)

    @pl.when(f == pl.num_programs(1) - 1)
    def _():
        y = alpha * x_ref[...] + gt_ref[0, 0] * acc_sc[...]
        o_ref[...] = _layer_norm(y, lg_ref[...], lb_ref[...])


def _mlp(x, modtab, layer, seq, mod_row_of_tile, w1, w2, ln_g, ln_b, alpha, *, tm_target=512, tf_target=1024):
    r, d = x.shape
    dff = w1.shape[1]
    tm = _tile(seq, tm_target)
    tf = _tile(dff, tf_target, 128)
    tiles_per_seq = seq // tm
    mod_idx = lambda i: mod_row_of_tile(i // tiles_per_seq)
    mod = lambda k: pl.BlockSpec((1, 1, 1, d), lambda i, f: (layer, mod_idx(i), 0, k))
    vec = pl.BlockSpec((1, d), lambda i, f: (0, 0))
    return pl.pallas_call(
        functools.partial(_mlp_body, alpha=alpha),
        out_shape=jax.ShapeDtypeStruct((r, d), F32),
        grid=(r // tm, dff // tf),
        in_specs=[pl.BlockSpec((tm, d), lambda i, f: (i, 0)), mod(3), mod(4), mod(5),
                  pl.BlockSpec((d, tf), lambda i, f: (0, f)),
                  pl.BlockSpec((tf, d), lambda i, f: (f, 0)), vec, vec],
        out_specs=pl.BlockSpec((tm, d), lambda i, f: (i, 0)),
        scratch_shapes=[pltpu.VMEM((tm, d), BF16), pltpu.VMEM((tm, d), F32)],
        compiler_params=_cparams("parallel", "arbitrary"),
        name="mlp_ln",
    )(x, modtab, modtab, modtab, w1, w2, ln_g.reshape(1, -1), ln_b.reshape(1, -1))


@functools.lru_cache(maxsize=None)
def _rope_tables(n):
    t = np.arange(n)
    row = (t // GRID_W).astype(np.float64)
    col = (t % GRID_W).astype(np.float64)
    n_pairs_axis = HEAD_DIM // 4
    inv_freq = ROPE_THETA ** (-np.arange(n_pairs_axis, dtype=np.float64) / n_pairs_axis)
    ang = np.concatenate([row[:, None] * inv_freq[None, :], col[:, None] * inv_freq[None, :]], axis=-1)
    cos = np.repeat(np.cos(ang), 2, axis=-1)
    sin = np.stack([-np.sin(ang), np.sin(ang)], axis=-1).reshape(n, HEAD_DIM)
    return cos.astype(np.float32), sin.astype(np.float32)


def kernel(x, c, ctx, c_ctx, w_mod, b_mod, w_in, q_norm_g, k_norm_g, hy_short_w, hy_short_b, hf_w1, hf_b1,
           hf_freq, hf_w2, hf_b2, hf_w3, hy_bias, nat_rpb, g_mix, w_out, ln1_g, ln1_b, w1, w2, ln2_g, ln2_b):
    bsz, n_lat, d = x.shape
    n_ctx = ctx.shape[1]
    depth = w_mod.shape[0]
    n_mix_heads = d // HEAD_DIM
    a_heads = n_mix_heads // 2
    hy = (n_mix_heads // 4) * HEAD_DIM
    c_heads = n_mix_heads - a_heads - hy // HEAD_DIM
    a_w, kv_w, c_w = a_heads * HEAD_DIM, A_KV_HEADS * HEAD_DIM, c_heads * HEAD_DIM
    attn_w = a_w + 2 * kv_w
    hy_col0 = attn_w
    nat_col0 = attn_w + 3 * hy
    assert w_in.shape[2] == nat_col0 + 3 * c_w and n_lat % GRID_W == 0
    alpha = (2 * depth) ** 0.25
    scale = HEAD_DIM ** -0.5 * LOG2E
    rows = n_lat // GRID_W
    kr = min((nat_rpb.shape[2] + 1) // 2, rows)
    ub_rows, ub_starts, ub_pids, nat_patterns = _nat_plan(rows, kr)

    n_rows = -(-(bsz + 1) // 8) * 8
    cc = jnp.zeros((n_rows, d), F32).at[:bsz].set(c).at[bsz].set(c_ctx)
    modtab = _modulation(cc, w_mod, b_mod).reshape(depth, n_rows, 1, 6 * d)
    lat_mod = lambda b: b
    ctx_mod = lambda b: bsz

    cos, sin = _rope_tables(n_lat)
    tabs_lat = _dft_tables(n_lat)
    spec_lat, mid_lat = _filter_spectrum(n_lat, hf_w1, hf_b1, hf_freq, hf_w2, hf_b2, hf_w3, tabs_lat,
                                         list(range(depth)))
    if depth > 1:
        tabs_ctx = _dft_tables(n_ctx)
        spec_ctx, mid_ctx = _filter_spectrum(n_ctx, hf_w1, hf_b1, hf_freq, hf_w2, hf_b2, hf_w3, tabs_ctx,
                                             list(range(depth - 1)))

    xl = x.reshape(bsz * n_lat, d)
    hc = ctx.reshape(bsz * n_ctx, d)
    for l in range(depth):
        keep_ctx = l < depth - 1
        w_in_l = _to_bf16(w_in, l)
        gq = q_norm_g[l].reshape(1, HEAD_DIM)
        gk = k_norm_g[l].reshape(1, HEAD_DIM)
        bias = _nat_bias(nat_rpb[l], kr, ub_rows, tuple(nat_patterns))

        proj_kw = dict(n_q=a_heads, n_k=A_KV_HEADS, plain_col0=hy_col0, nat_col0=nat_col0, n_nat_q=c_heads,
                       scale=scale)
        proj = _inproj(xl, modtab, l, n_lat, lat_mod, w_in_l, gq, gk, cos=cos, sin=sin, **proj_kw)
        cproj = _inproj(hc, modtab, l, n_ctx, ctx_mod, w_in_l, gq, gk, **proj_kw)

        o_a = _attention(proj, 0, a_heads, A_KV_HEADS,
                         [(proj, a_w, a_w + kv_w, n_lat), (cproj, a_w, a_w + kv_w, n_ctx)], bsz, n_lat)
        o_b = _hyena(proj, hy_col0, hy, hy_short_w[l], hy_short_b[l], spec_lat, mid_lat, l, hy_bias[l], tabs_lat,
                     bsz, n_lat)
        o_c = _natten(proj, cproj, nat_col0, bias, ub_starts, ub_pids, bsz, n_lat, n_ctx, c_heads)
        w_out_l = _to_bf16(w_out, l)
        xl_new = _merge(o_a, o_b, o_c, xl, modtab, l, n_lat, lat_mod, g_mix[l], w_out_l, ln1_g[l], ln1_b[l], alpha)

        w1_l = _to_bf16(w1, l)
        w2_l = _to_bf16(w2, l)
        if keep_ctx:
            co_a = _attention(cproj, 0, a_heads, A_KV_HEADS, [(cproj, a_w, a_w + kv_w, n_ctx)], bsz, n_ctx)
            co_b = _hyena(cproj, hy_col0, hy, hy_short_w[l], hy_short_b[l], spec_ctx, mid_ctx, l, hy_bias[l],
                          tabs_ctx, bsz, n_ctx)
            co_c = _attention(cproj, nat_col0, c_heads, c_heads,
                              [(cproj, nat_col0 + c_w, nat_col0 + 2 * c_w, n_ctx)], bsz, n_ctx)
            hc = _merge(co_a, co_b, co_c, hc, modtab, l, n_ctx, ctx_mod, g_mix[l], w_out_l, ln1_g[l], ln1_b[l], alpha)
            hc = _mlp(hc, modtab, l, bsz * n_ctx, ctx_mod, w1_l, w2_l, ln2_g[l], ln2_b[l], alpha)
        xl = _mlp(xl_new, modtab, l, n_lat, lat_mod, w1_l, w2_l, ln2_g[l], ln2_b[l], alpha)
    return xl.reshape(bsz, n_lat, d)
```

```python
import functools
import math

import jax
import jax.numpy as jnp
import numpy as np
from jax import lax
from jax.experimental import pallas as pl
from jax.experimental.pallas import tpu as pltpu

F32 = jnp.float32
BF16 = jnp.bfloat16

HEAD_DIM = 128
GRID_W = 64
A_KV_HEADS = 2
ROPE_THETA = 10000.0
HY_ORDER = 2
HY_DECAY_TARGET = 1e-2
HY_DECAY_SHORT_PCT = 0.3
HY_DECAY_LONG_PCT = 1.5
HY_DECAY_SHIFT = 0.05
EPS = 1e-6
NEG_INF = -1e30
LOG2E = math.log2(math.e)
VMEM_LIMIT_BYTES = 56 * 1024 * 1024


def _cparams(*sem):
    return pltpu.CompilerParams(dimension_semantics=sem, vmem_limit_bytes=VMEM_LIMIT_BYTES)


def _tile(n, target, mult=8):
    if n <= target:
        return n
    t = (target // mult) * mult
    while t >= mult:
        if n % t == 0:
            return t
        t -= mult
    return n


def _cast_body(w_ref, o_ref):
    o_ref[...] = w_ref[0].astype(o_ref.dtype)


def _to_bf16(w, layer):
    _, r, c = w.shape
    tr = _tile(r, max(8, (8 * 1024 * 1024) // (4 * c)), 16)
    return pl.pallas_call(
        _cast_body,
        out_shape=jax.ShapeDtypeStruct((r, c), BF16),
        grid=(r // tr,),
        in_specs=[pl.BlockSpec((1, tr, c), lambda i: (layer, i, 0))],
        out_specs=pl.BlockSpec((tr, c), lambda i: (i, 0)),
        compiler_params=_cparams("parallel"),
        name="weight_to_bf16",
    )(w)


def _mod_body(c_ref, w_ref, b_ref, o_ref):
    c = c_ref[...]
    s = (c / (1.0 + jnp.exp(-c))).astype(BF16)
    o_ref[0] = jnp.dot(s, w_ref[0].astype(BF16), preferred_element_type=F32) + b_ref[0]


def _modulation(cc, w_mod, b_mod):
    nl, d, n6 = w_mod.shape
    r = cc.shape[0]
    tn = _tile(n6, 1536, 128)
    return pl.pallas_call(
        _mod_body,
        out_shape=jax.ShapeDtypeStruct((nl, r, n6), F32),
        grid=(nl, n6 // tn),
        in_specs=[pl.BlockSpec((r, d), lambda l, j: (0, 0)),
                  pl.BlockSpec((1, d, tn), lambda l, j: (l, 0, j)),
                  pl.BlockSpec((1, 1, tn), lambda l, j: (l, 0, j))],
        out_specs=pl.BlockSpec((1, r, tn), lambda l, j: (l, 0, j)),
        compiler_params=_cparams("parallel", "parallel"),
        name="modulation",
    )(cc, w_mod, b_mod.reshape(nl, 1, n6))


def _rms_head(xh, g):
    return xh * lax.rsqrt(jnp.mean(xh * xh, axis=-1, keepdims=True) + EPS) * g


def _rope_head(y, cos, sin):
    lane = lax.broadcasted_iota(jnp.int32, y.shape, 1)
    swapped = jnp.where((lane & 1) == 0, pltpu.roll(y, HEAD_DIM - 1, 1), pltpu.roll(y, 1, 1))
    return y * cos + swapped * sin


def _inproj_body(*refs, n_q, n_k, plain_col0, nat_col0, n_nat_q, rope, scale):
    x_ref, sh_ref, sc_ref, w_ref, gq_ref, gk_ref = refs[:6]
    o_ref = refs[-1]
    width = w_ref.shape[1]
    u = (x_ref[...] * (1.0 + sc_ref[0, 0]) + sh_ref[0, 0]).astype(BF16)
    acc_attn = jnp.dot(u, w_ref[:, :plain_col0], preferred_element_type=F32)
    acc_plain = jnp.dot(u, w_ref[:, plain_col0:nat_col0], preferred_element_type=F32)
    acc_nat = jnp.dot(u, w_ref[:, nat_col0:], preferred_element_type=F32)
    gq = gq_ref[...] * scale
    gk = gk_ref[...]
    if rope:
        cos = refs[6][...]
        sin = refs[7][...]
    for hb in range(plain_col0 // HEAD_DIM):
        xh = acc_attn[:, hb * HEAD_DIM:(hb + 1) * HEAD_DIM]
        if hb < n_q + n_k:
            xh = _rms_head(xh, gq if hb < n_q else gk)
            if rope:
                xh = _rope_head(xh, cos, sin)
        o_ref[:, hb * HEAD_DIM:(hb + 1) * HEAD_DIM] = xh.astype(o_ref.dtype)
    o_ref[:, plain_col0:nat_col0] = acc_plain.astype(o_ref.dtype)
    for hb in range((width - nat_col0) // HEAD_DIM):
        xh = acc_nat[:, hb * HEAD_DIM:(hb + 1) * HEAD_DIM]
        if hb < n_nat_q:
            xh = xh * scale
        o_ref[:, nat_col0 + hb * HEAD_DIM:nat_col0 + (hb + 1) * HEAD_DIM] = xh.astype(o_ref.dtype)


def _inproj(x, modtab, layer, seq, mod_row_of_tile, w, gq, gk, *, n_q, n_k, plain_col0, nat_col0, n_nat_q,
            cos=None, sin=None, scale=1.0, tm_target=512):
    r, d = x.shape
    width = w.shape[1]
    tm = _tile(seq, tm_target)
    tiles_per_seq = seq // tm
    rope = cos is not None
    mod_idx = lambda i: mod_row_of_tile(i // tiles_per_seq)
    in_specs = [pl.BlockSpec((tm, d), lambda i: (i, 0)),
                pl.BlockSpec((1, 1, 1, d), lambda i: (layer, mod_idx(i), 0, 0)),
                pl.BlockSpec((1, 1, 1, d), lambda i: (layer, mod_idx(i), 0, 1)),
                pl.BlockSpec((d, width), lambda i: (0, 0)),
                pl.BlockSpec((1, HEAD_DIM), lambda i: (0, 0)), pl.BlockSpec((1, HEAD_DIM), lambda i: (0, 0))]
    args = [x, modtab, modtab, w, gq, gk]
    if rope:
        in_specs += [pl.BlockSpec((tm, HEAD_DIM), lambda i: (i % tiles_per_seq, 0))] * 2
        args += [cos, sin]
    body = functools.partial(_inproj_body, n_q=n_q, n_k=n_k, plain_col0=plain_col0, nat_col0=nat_col0,
                             n_nat_q=n_nat_q, rope=rope, scale=scale)
    return pl.pallas_call(
        body,
        out_shape=jax.ShapeDtypeStruct((r, width), BF16),
        grid=(r // tm,),
        in_specs=in_specs,
        out_specs=pl.BlockSpec((tm, width), lambda i: (i, 0)),
        compiler_params=_cparams("parallel"),
        name="inproj",
    )(*args)


def _softmax_pv_t(s, vs):
    m = jnp.max(s[0], axis=0, keepdims=True)
    for si in s[1:]:
        m = jnp.maximum(m, jnp.max(si, axis=0, keepdims=True))
    p = [jnp.exp2(si - m) for si in s]
    l = jnp.sum(p[0], axis=0, keepdims=True)
    for pi in p[1:]:
        l = l + jnp.sum(pi, axis=0, keepdims=True)
    ot = None
    for v, pi in zip(vs, p):
        part = lax.dot_general(v, pi.astype(BF16), (((0,), (0,)), ((), ())), preferred_element_type=F32)
        ot = part if ot is None else ot + part
    return ot * (1.0 / l)


def _attn_body(*refs, g, n_src, n_sub):
    q_ref = refs[0]
    k_refs = refs[1:1 + n_src]
    v_refs = refs[1 + n_src:1 + 2 * n_src]
    o_ref = refs[-1]
    tq = q_ref.shape[0]
    hs = g // n_sub
    scores = []
    for j in range(n_sub):
        heads = range(j * hs, (j + 1) * hs)
        qs = jnp.concatenate([q_ref[:, h * HEAD_DIM:(h + 1) * HEAD_DIM] for h in heads], axis=0)
        scores.append([lax.dot_general(k_ref[...], qs, (((1,), (1,)), ((), ())), preferred_element_type=F32)
                       for k_ref in k_refs])
    for j in range(n_sub):
        ot = _softmax_pv_t(scores[j], [v_ref[...] for v_ref in v_refs])
        for i in range(hs):
            h = j * hs + i
            o_ref[:, h * HEAD_DIM:(h + 1) * HEAD_DIM] = ot[:, i * tq:(i + 1) * tq].T.astype(o_ref.dtype)


def _attention(qarr, q_col0, n_q_heads, n_kv_heads, srcs, n_seq, seq_q, *, tq_target=512):
    g = n_q_heads // n_kv_heads
    gw = g * HEAD_DIM
    assert q_col0 % gw == 0
    tq = _tile(seq_q, tq_target, 128)
    tiles = seq_q // tq
    q_cb = q_col0 // gw
    in_specs = [pl.BlockSpec((tq, gw), lambda b, h, t: (b * tiles + t, q_cb + h))]
    args = [qarr]
    for which in (1, 2):
        for src in srcs:
            arr, col0, nk = src[0], src[which], src[3]
            cb = col0 // HEAD_DIM
            in_specs.append(pl.BlockSpec((nk, HEAD_DIM), lambda b, h, t, cb=cb: (b, cb + h)))
            args.append(arr)
    return pl.pallas_call(
        functools.partial(_attn_body, g=g, n_src=len(srcs), n_sub=g),
        out_shape=jax.ShapeDtypeStruct((n_seq * seq_q, n_q_heads * HEAD_DIM), F32),
        grid=(n_seq, n_kv_heads, tiles),
        in_specs=in_specs,
        out_specs=pl.BlockSpec((tq, gw), lambda b, h, t: (b * tiles + t, h)),
        compiler_params=_cparams("parallel", "parallel", "parallel"),
        name="attention",
    )(*args)


NAT_GROUP_ROWS = 4


def _nat_plan(rows, kr):
    ub_rows = min(kr + NAT_GROUP_ROWS, rows)
    assert rows % NAT_GROUP_ROWS == 0
    starts, pids, patterns = [], [], []
    for r0 in range(0, rows, NAT_GROUP_ROWS):
        band = lambda r: min(max(r - kr // 2, 0), rows - kr)
        ub = min(band(r0), rows - ub_rows)
        pat = tuple((band(r) - ub, r - ub) for r in range(r0, r0 + NAT_GROUP_ROWS))
        if pat not in patterns:
            patterns.append(pat)
        starts.append(ub)
        pids.append(patterns.index(pat))
    return ub_rows, starts, pids, patterns


def _nat_bias_body(rpb_ref, o_ref, *, win_r, win_c, kr, ub_rows, patterns):
    h = pl.program_id(0)
    w = GRID_W
    ck = lax.broadcasted_iota(jnp.int32, (w, w), 0)
    cq = lax.broadcasted_iota(jnp.int32, (w, w), 1)
    c_start = jnp.clip(cq - win_c // 2, 0, w - win_c)
    inside = (ck >= c_start) & (ck < c_start + win_c)
    d_col = ck - cq + (win_c - 1)
    n_dr = 2 * win_r - 1
    n_dc = 2 * win_c - 1
    tiles = []
    for dr in range(n_dr):
        t = jnp.zeros((w, w), F32)
        for dc in range(n_dc):
            t = jnp.where(d_col == dc, rpb_ref[(h * n_dr + dr) * n_dc + dc], t)
        tiles.append(jnp.where(inside, t * LOG2E, NEG_INF))
    masked = jnp.full((w, w), NEG_INF, F32)
    for p, pat in enumerate(patterns):
        for rq, (s_rel, r_rel) in enumerate(pat):
            for kk in range(ub_rows):
                blk = tiles[kk - r_rel + win_r - 1] if s_rel <= kk < s_rel + kr else masked
                o_ref[0, p, kk * w:(kk + 1) * w, rq * w:(rq + 1) * w] = blk


def _nat_bias(rpb, kr, ub_rows, patterns):
    nh, n_dr, n_dc = rpb.shape
    win_r, win_c = (n_dr + 1) // 2, (n_dc + 1) // 2
    shape = (len(patterns), ub_rows * GRID_W, NAT_GROUP_ROWS * GRID_W)
    return pl.pallas_call(
        functools.partial(_nat_bias_body, win_r=win_r, win_c=win_c, kr=kr, ub_rows=ub_rows, patterns=patterns),
        out_shape=jax.ShapeDtypeStruct((nh,) + shape, F32),
        grid=(nh,),
        in_specs=[pl.BlockSpec(memory_space=pltpu.SMEM)],
        out_specs=pl.BlockSpec((1,) + shape, lambda h: (h, 0, 0, 0)),
        compiler_params=_cparams("parallel"),
        name="nat_bias",
    )(rpb.reshape(-1))


def _nat_body(q_ref, k_ref, v_ref, kc_ref, vc_ref, bias_ref, o_ref, *, starts, pids):
    w = GRID_W
    gq = NAT_GROUP_ROWS * w
    nk = bias_ref.shape[2]
    nt = (((1,), (1,)), ((), ()))
    scores = []
    for g, (ub, pid) in enumerate(zip(starts, pids)):
        q = q_ref[g * gq:(g + 1) * gq, :]
        s_loc = lax.dot_general(k_ref[ub * w:ub * w + nk, :], q, nt, preferred_element_type=F32)
        s_ctx = lax.dot_general(kc_ref[...], q, nt, preferred_element_type=F32)
        scores.append([s_loc + bias_ref[0, pid], s_ctx])
    for g, ub in enumerate(starts):
        ot = _softmax_pv_t(scores[g], [v_ref[ub * w:ub * w + nk, :], vc_ref[...]])
        o_ref[g * gq:(g + 1) * gq, :] = ot.T.astype(o_ref.dtype)


def _natten(proj, cproj, col0, bias, starts, pids, n_seq, seq, n_ctx, n_heads):
    hb0 = col0 // HEAD_DIM
    blk = lambda part: pl.BlockSpec((seq, HEAD_DIM), lambda h, b: (b, hb0 + part * n_heads + h))
    cblk = lambda part: pl.BlockSpec((n_ctx, HEAD_DIM), lambda h, b: (b, hb0 + part * n_heads + h))
    return pl.pallas_call(
        functools.partial(_nat_body, starts=tuple(starts), pids=tuple(pids)),
        out_shape=jax.ShapeDtypeStruct((n_seq * seq, n_heads * HEAD_DIM), F32),
        grid=(n_heads, n_seq),
        in_specs=[blk(0), blk(1), blk(2), cblk(1), cblk(2),
                  pl.BlockSpec((1,) + bias.shape[1:], lambda h, b: (h, 0, 0, 0))],
        out_specs=pl.BlockSpec((seq, HEAD_DIM), lambda h, b: (b, h)),
        compiler_params=_cparams("parallel", "parallel"),
        name="natten",
    )(proj, proj, proj, cproj, cproj, bias)


LANES = 128


@functools.lru_cache(maxsize=None)
def _dft_tables(n):
    h = n // 2
    k = np.arange(h, dtype=np.int64)
    even = ((k[:, None] * (2 * k[None, :])) % (2 * n)).astype(np.float64) * (math.pi / n)
    odd = ((k[:, None] * (2 * k[None, :] + 1)) % (2 * n)).astype(np.float64) * (math.pi / n)
    to_bf16 = lambda a: np.asarray(a, np.float32).astype(BF16)
    return tuple(to_bf16(a) for a in (np.cos(even), np.sin(even), np.cos(odd), np.sin(odd),
                                      np.cos(odd).T, np.sin(odd).T))


@functools.lru_cache(maxsize=None)
def _filter_consts(n, n_bands, hy):
    t = np.linspace(0.0, 1.0, n)
    bands = np.arange(1, n_bands + 1, dtype=np.float64)
    ang = 2.0 * math.pi * t[:, None] * bands[None, :]
    feat = np.concatenate([t[:, None], np.cos(ang), np.sin(ang)], axis=-1)
    max_decay = math.log(HY_DECAY_TARGET) / HY_DECAY_SHORT_PCT
    min_decay = math.log(HY_DECAY_TARGET) / HY_DECAY_LONG_PCT
    deltas = np.abs(np.linspace(min_decay, max_decay, hy))
    window = np.exp(-t[:, None] * deltas[None, :]) + HY_DECAY_SHIFT
    return feat.astype(np.float32), window.astype(np.float32)


def _to_slabs(slab_ref, x):
    for s in range(slab_ref.shape[0]):
        slab_ref[s] = x[:, s * LANES:(s + 1) * LANES]


def _even_odd(slab_ref):
    ns, n, _ = slab_ref.shape
    cat = lambda parts: parts[0] if ns == 1 else jnp.concatenate(parts, axis=1)
    ev = cat([slab_ref[s, pl.ds(0, n // 2, stride=2), :] for s in range(ns)])
    od = cat([slab_ref[s, pl.ds(1, n // 2, stride=2), :] for s in range(ns)])
    return ev, od


def _alt_sign(h):
    j = lax.broadcasted_iota(jnp.int32, (h, 1), 0)
    return jnp.where((j & 1) == 0, 1.0, -1.0)


def _dot(a, b):
    return jnp.dot(a, b, preferred_element_type=F32)


def _filter_body(feat_ref, w1_ref, b1_ref, fr_ref, w2_ref, b2_ref, w3_ref, win_ref,
                 ce_ref, se_ref, co_ref, so_ref, spec_ref, mid_ref, hhi_sc, hlo_sc, ge_sc, go_sc, *, n, hy):
    j = pl.program_id(1)
    hp = lax.Precision.HIGHEST
    half = n // 2
    cw = win_ref.shape[1]

    @pl.when(j == 0)
    def _():
        fr = fr_ref[0]
        hid = jnp.sin(fr * (jnp.dot(feat_ref[...], w1_ref[0], precision=hp, preferred_element_type=F32) + b1_ref[0]))
        hid = jnp.sin(fr * (jnp.dot(hid, w2_ref[0], precision=hp, preferred_element_type=F32) + b2_ref[0]))
        hhi_sc[...] = hid.astype(BF16)
        hlo_sc[...] = (hid - hid.astype(BF16).astype(F32)).astype(BF16)

    def last_layer(w):
        w_hi = w.astype(BF16)
        w_lo = (w - w_hi.astype(F32)).astype(BF16)
        return _dot(hhi_sc[...], w_hi) + (_dot(hhi_sc[...], w_lo) + _dot(hlo_sc[...], w_hi))

    win = win_ref[...]
    t_idx = lax.broadcasted_iota(jnp.int32, (n, cw), 0)
    alt = _alt_sign(half)
    k_idx = lax.broadcasted_iota(jnp.int32, (half, cw), 0)
    wk = jnp.where(k_idx == 0, 1.0 / (2 * n), 2.0 / (2 * n))
    for o in range(HY_ORDER):
        fwd = last_layer(w3_ref[0, 2 * o]) * win
        bwd = last_layer(w3_ref[0, 2 * o + 1]) * win
        bwd = jnp.where(t_idx == 0, 0.0, bwd)
        inv = 1.0 / (jnp.sum(jnp.abs(fwd), axis=0, keepdims=True)
                     + jnp.sum(jnp.abs(bwd), axis=0, keepdims=True) + EPS)
        _to_slabs(ge_sc, (fwd + bwd) * inv)
        _to_slabs(go_sc, (fwd - bwd) * inv)
        ge_e, ge_o = _even_odd(ge_sc)
        go_e, go_o = _even_odd(go_sc)
        ac = _dot(ce_ref[...], ge_e.astype(BF16))
        bc = _dot(co_ref[...], ge_o.astype(BF16))
        a_s = _dot(se_ref[...], go_e.astype(BF16))
        bs = _dot(so_ref[...], go_o.astype(BF16))
        spec_ref[0, o, 0] = (ac + bc) * wk
        spec_ref[0, o, 1] = (a_s + bs) * wk
        spec_ref[0, o, 2] = (ac - bc) * wk
        spec_ref[0, o, 3] = (bs - a_s) * wk
        mid_r = jnp.sum(ge_e * alt, axis=0, keepdims=True) * (2.0 / (2 * n))
        mid_s = jnp.sum(go_o * alt, axis=0, keepdims=True) * (2.0 / (2 * n))
        row = lax.broadcasted_iota(jnp.int32, (8, cw), 0)
        mid_ref[0, o] = jnp.where(row == 0, mid_r, jnp.where(row == 1, mid_s, 0.0))


def _filter_spectrum(n, hf_w1, hf_b1, hf_freq, hf_w2, hf_b2, hf_w3, tabs, layers):
    nl = len(layers)
    n_pos, n_ffn = hf_w1.shape[1:]
    hy = hf_w3.shape[2] // (2 * HY_ORDER)
    half = n // 2
    cw = LANES
    feat, window = _filter_consts(n, (n_pos - 1) // 2, hy)
    lsel = jnp.asarray(layers, jnp.int32)
    lanes = lambda k: -(-k // LANES) * LANES
    kp, fp = lanes(n_pos), lanes(n_ffn)
    feat = np.pad(feat, ((0, 0), (0, kp - n_pos)))
    pick = lambda a, r, c: jnp.pad(a[lsel], ((0, 0), (0, r - a.shape[1]), (0, c - a.shape[2])))
    vec = lambda a: jnp.pad(a[lsel], ((0, 0), (0, fp - a.shape[1]))).reshape(nl, 1, fp)
    w3 = pick(hf_w3, fp, hf_w3.shape[2]).reshape(nl, fp, 2 * HY_ORDER, hy).transpose(0, 2, 1, 3)
    full = lambda shp: pl.BlockSpec((1,) + shp, lambda l, j: (l, 0, 0))
    tab = pl.BlockSpec((half, half), lambda l, j: (0, 0))
    return pl.pallas_call(
        functools.partial(_filter_body, n=n, hy=hy),
        out_shape=(jax.ShapeDtypeStruct((nl, HY_ORDER, 4, half, hy), F32),
                   jax.ShapeDtypeStruct((nl, HY_ORDER, 8, hy), F32)),
        grid=(nl, hy // cw),
        in_specs=[pl.BlockSpec((n, kp), lambda l, j: (0, 0)),
                  full((kp, fp)), full((1, fp)), full((1, fp)), full((fp, fp)), full((1, fp)),
                  pl.BlockSpec((1, 2 * HY_ORDER, fp, cw), lambda l, j: (l, 0, 0, j)),
                  pl.BlockSpec((n, cw), lambda l, j: (0, j)),
                  tab, tab, tab, tab],
        out_specs=(pl.BlockSpec((1, HY_ORDER, 4, half, cw), lambda l, j: (l, 0, 0, 0, j)),
                   pl.BlockSpec((1, HY_ORDER, 8, cw), lambda l, j: (l, 0, 0, j))),
        scratch_shapes=[pltpu.VMEM((n, fp), BF16), pltpu.VMEM((n, fp), BF16),
                        pltpu.VMEM((cw // LANES, n, LANES), F32), pltpu.VMEM((cw // LANES, n, LANES), F32)],
        compiler_params=_cparams("parallel", "arbitrary"),
        name="hyena_filter",
    )(feat, pick(hf_w1, kp, fp), vec(hf_b1), vec(hf_freq), pick(hf_w2, fp, fp), vec(hf_b2), w3, window,
      *tabs[:4])


def _conv3(x, w_ref, b_ref):
    n = x.shape[0]
    t = lax.broadcasted_iota(jnp.int32, x.shape, 0)
    prev = jnp.where(t == 0, 0.0, pltpu.roll(x, 1, 0))
    nxt = jnp.where(t == n - 1, 0.0, pltpu.roll(x, n - 1, 0))
    return b_ref[...] + w_ref[0:1, :] * prev + w_ref[1:2, :] * x + w_ref[2:3, :] * nxt


def _hyena_body(*refs, conv_z):
    ce_ref, se_ref, co_ref, so_ref, cot_ref, sot_ref, z_ref, gate_ref = refs[:8]
    i = 8
    if conv_z:
        wz_ref, bz_ref = refs[i:i + 2]
        i += 2
    wg_ref, bg_ref, skip_ref, spec_ref, mid_ref, o_ref, z_sc, y_sc = refs[i:]
    half = ce_ref.shape[0]
    z = z_ref[...].astype(F32)
    if conv_z:
        z = _conv3(z, wz_ref, bz_ref)
    _to_slabs(z_sc, z)
    ze, zo = _even_odd(z_sc)
    ze16, zo16 = ze.astype(BF16), zo.astype(BF16)
    ac, bc = _dot(ce_ref[...], ze16), _dot(co_ref[...], zo16)
    a_s, bs = _dot(se_ref[...], ze16), _dot(so_ref[...], zo16)
    alt = _alt_sign(half)

    def cmul(zr, zs, hr, hs):
        return zr * hr - zs * hs, zr * hs + zs * hr

    yr_lo, ys_lo = cmul(ac + bc, a_s + bs, spec_ref[0, 0, 0], spec_ref[0, 0, 1])
    yr_hi, ys_hi = cmul(ac - bc, bs - a_s, spec_ref[0, 0, 2], spec_ref[0, 0, 3])
    yr_mid, ys_mid = cmul(jnp.sum(ze * alt, axis=0, keepdims=True), jnp.sum(zo * alt, axis=0, keepdims=True),
                          mid_ref[0, 0, 0:1, :], mid_ref[0, 0, 1:2, :])
    y_e = (_dot(ce_ref[...], (yr_lo + yr_hi).astype(BF16)) + _dot(se_ref[...], (ys_lo - ys_hi).astype(BF16))
           + alt * yr_mid)
    y_o = (_dot(cot_ref[...], (yr_lo - yr_hi).astype(BF16)) + _dot(sot_ref[...], (ys_lo + ys_hi).astype(BF16))
           + alt * ys_mid)
    skip = skip_ref[0]
    y_e = y_e + skip * ze
    y_o = y_o + skip * zo
    for s in range(y_sc.shape[0]):
        y_sc[s, pl.ds(0, half, stride=2), :] = y_e[:, s * LANES:(s + 1) * LANES]
        y_sc[s, pl.ds(1, half, stride=2), :] = y_o[:, s * LANES:(s + 1) * LANES]
    gate = _conv3(gate_ref[...].astype(F32), wg_ref, bg_ref)
    for s in range(y_sc.shape[0]):
        o_ref[:, s * LANES:(s + 1) * LANES] = gate[:, s * LANES:(s + 1) * LANES] * y_sc[s]


def _hyena_order(z_arr, z_cb, hyp, hyp_col0, gate_part, short_w, short_b, spec, mid, li, order, skip, tabs,
                 n_seq, seq, hy, *, conv_z):
    half = seq // 2
    cw = min(hy, 2 * LANES)
    cpb = hy // cw
    assert hyp_col0 % cw == 0
    hcb = hyp_col0 // cw
    tab = pl.BlockSpec((half, half), lambda b, j: (0, 0))
    part = lambda p, rows: pl.BlockSpec((rows, cw), lambda b, j: (0, p * cpb + j))
    in_specs = [tab] * 6 + [pl.BlockSpec((seq, cw), lambda b, j: (b, z_cb + j)),
                            pl.BlockSpec((seq, cw), lambda b, j: (b, hcb + gate_part * cpb + j))]
    args = list(tabs) + [z_arr, hyp]
    if conv_z:
        in_specs += [part(0, short_w.shape[0]), part(0, 1)]
        args += [short_w, short_b.reshape(1, -1)]
    in_specs += [part(gate_part, short_w.shape[0]), part(gate_part, 1),
                 pl.BlockSpec((1, 1, cw), lambda b, j: (order, 0, j)),
                 pl.BlockSpec((1, 1, 4, half, cw), lambda b, j: (li, order, 0, 0, j)),
                 pl.BlockSpec((1, 1, 8, cw), lambda b, j: (li, order, 0, j))]
    args += [short_w, short_b.reshape(1, -1), skip.reshape(HY_ORDER, 1, hy), spec, mid]
    return pl.pallas_call(
        functools.partial(_hyena_body, conv_z=conv_z),
        out_shape=jax.ShapeDtypeStruct((n_seq * seq, hy), F32),
        grid=(n_seq, cpb),
        in_specs=in_specs,
        out_specs=pl.BlockSpec((seq, cw), lambda b, j: (b, j)),
        scratch_shapes=[pltpu.VMEM((cw // LANES, seq, LANES), F32), pltpu.VMEM((cw // LANES, seq, LANES), F32)],
        compiler_params=_cparams("parallel", "parallel"),
        name="hyena_order",
    )(*args)


def _hyena(proj, col0, hy, short_w, short_b, spec, mid, li, skip, tabs, n_seq, seq):
    kw = dict(short_w=short_w, short_b=short_b, spec=spec, mid=mid, li=li, skip=skip, tabs=tabs,
              n_seq=n_seq, seq=seq, hy=hy)
    cw = min(hy, 2 * LANES)
    z = _hyena_order(proj, col0 // cw, proj, col0, 1, order=0, conv_z=True, **kw)
    return _hyena_order(z, 0, proj, col0, 2, order=1, conv_z=False, **kw)


def _layer_norm(y, g, b):
    mu = jnp.mean(y, axis=-1, keepdims=True)
    yc = y - mu
    var = jnp.mean(yc * yc, axis=-1, keepdims=True)
    return yc * lax.rsqrt(var + EPS) * g + b


def _merge_body(oa_ref, ob_ref, oc_ref, x_ref, gt_ref, sh2_ref, sc2_ref, g_ref, w_ref, lg_ref, lb_ref,
                o_ref, u_ref, mix_sc, *, alpha, n_sub):
    tm = x_ref.shape[0]
    rs = tm // n_sub
    mixes = []
    for j in range(n_sub):
        rows = slice(j * rs, (j + 1) * rs)
        col = 0
        for ref in (oa_ref, ob_ref, oc_ref):
            v = ref[rows, :]
            wd = v.shape[1]
            vn = v * lax.rsqrt(jnp.mean(v * v, axis=-1, keepdims=True) + EPS)
            mix_sc[rows, col:col + wd] = (vn * g_ref[:, col:col + wd]).astype(BF16)
            col += wd
        mixes.append(jnp.dot(mix_sc[rows, :], w_ref[...], preferred_element_type=F32))
    for j in range(n_sub):
        rows = slice(j * rs, (j + 1) * rs)
        y = alpha * x_ref[rows, :] + gt_ref[0, 0] * mixes[j]
        xn = _layer_norm(y, lg_ref[...], lb_ref[...])
        o_ref[rows, :] = xn
        u_ref[rows, :] = (xn * (1.0 + sc2_ref[0, 0]) + sh2_ref[0, 0]).astype(u_ref.dtype)


def _merge(oa, ob, oc, x, modtab, layer, seq, mod_row_of_tile, g_mix, w_out, ln_g, ln_b, alpha, *, tm_target=512):
    r, d = x.shape
    tm = _tile(seq, tm_target)
    tiles_per_seq = seq // tm
    mod_idx = lambda i: mod_row_of_tile(i // tiles_per_seq)
    mod = lambda k: pl.BlockSpec((1, 1, 1, d), lambda i: (layer, mod_idx(i), 0, k))
    row = lambda a: pl.BlockSpec((tm, a.shape[1]), lambda i: (i, 0))
    vec = pl.BlockSpec((1, d), lambda i: (0, 0))
    return pl.pallas_call(
        functools.partial(_merge_body, alpha=alpha, n_sub=2 if tm % 32 == 0 else 1),
        out_shape=(jax.ShapeDtypeStruct((r, d), F32), jax.ShapeDtypeStruct((r, d), BF16)),
        grid=(r // tm,),
        in_specs=[row(oa), row(ob), row(oc), row(x), mod(2), mod(3), mod(4),
                  vec, pl.BlockSpec(w_out.shape, lambda i: (0, 0)), vec, vec],
        out_specs=(pl.BlockSpec((tm, d), lambda i: (i, 0)), pl.BlockSpec((tm, d), lambda i: (i, 0))),
        scratch_shapes=[pltpu.VMEM((tm, w_out.shape[0]), BF16)],
        compiler_params=_cparams("parallel"),
        name="merge_outproj_ln",
    )(oa, ob, oc, x, modtab, modtab, modtab, g_mix.reshape(1, -1), w_out, ln_g.reshape(1, -1), ln_b.reshape(1, -1))


def _mlp_body(u_ref, x_ref, gt_ref, w1_ref, w2_ref, lg_ref, lb_ref, o_ref, acc_sc, *, alpha, n_tiles):
    i = pl.program_id(0)
    f = pl.program_id(1)
    rc = o_ref.shape[0]

    def mlp_chunk():
        h = jnp.maximum(jnp.dot(u_ref[...], w1_ref[...], preferred_element_type=F32), 0.0)
        return jnp.dot((h * h).astype(BF16), w2_ref[...], preferred_element_type=F32)

    def finish_rows():
        rows = pl.ds(pl.multiple_of(f * rc, rc), rc)
        y = alpha * x_ref[...] + gt_ref[0, 0] * acc_sc[1, rows, :]
        o_ref[...] = _layer_norm(y, lg_ref[...], lb_ref[...])

    @pl.when((i == 0) & (f == 0))
    def _():
        acc_sc[0] = jnp.zeros(acc_sc.shape[1:], F32)

    @pl.when(i == 0)
    def _():
        acc_sc[0] += mlp_chunk()

    @pl.when((i > 0) & (i < n_tiles))
    def _():
        part = mlp_chunk()
        finish_rows()
        acc_sc[0] += part

    @pl.when(i == n_tiles)
    def _():
        finish_rows()

    @pl.when((f == pl.num_programs(1) - 1) & (i < n_tiles))
    def _():
        acc_sc[1] = acc_sc[0]
        acc_sc[0] = jnp.zeros(acc_sc.shape[1:], F32)


def _mlp(u, x, modtab, layer, seq, mod_row_of_tile, w1, w2, ln_g, ln_b, alpha, *, tm_target=512, tf_target=2048):
    r, d = x.shape
    dff = w1.shape[1]
    tm = _tile(seq, tm_target)
    tf = _tile(dff, tf_target, 128)
    nt, nf = r // tm, dff // tf
    rc = tm // nf
    assert tm % nf == 0 and rc % 8 == 0
    tiles_per_seq = seq // tm
    prev = lambda i: jnp.maximum(i - 1, 0)
    chunk = lambda i, f: prev(i) * nf + jnp.where(i > 0, f, 0)
    wchunk = lambda i, f: jnp.where(i < nt, f, nf - 1)
    vec = pl.BlockSpec((1, d), lambda i, f: (0, 0))
    return pl.pallas_call(
        functools.partial(_mlp_body, alpha=alpha, n_tiles=nt),
        out_shape=jax.ShapeDtypeStruct((r, d), F32),
        grid=(nt + 1, nf),
        in_specs=[pl.BlockSpec((tm, d), lambda i, f: (jnp.minimum(i, nt - 1), 0)),
                  pl.BlockSpec((rc, d), lambda i, f: (chunk(i, f), 0)),
                  pl.BlockSpec((1, 1, 1, d), lambda i, f: (layer, mod_row_of_tile(prev(i) // tiles_per_seq), 0, 5)),
                  pl.BlockSpec((d, tf), lambda i, f: (0, wchunk(i, f))),
                  pl.BlockSpec((tf, d), lambda i, f: (wchunk(i, f), 0)), vec, vec],
        out_specs=pl.BlockSpec((rc, d), lambda i, f: (chunk(i, f), 0)),
        scratch_shapes=[pltpu.VMEM((2, tm, d), F32)],
        compiler_params=_cparams("arbitrary", "arbitrary"),
        name="mlp_ln",
    )(u, x, modtab, w1, w2, ln_g.reshape(1, -1), ln_b.reshape(1, -1))


@functools.lru_cache(maxsize=None)
def _rope_tables(n):
    t = np.arange(n)
    row = (t // GRID_W).astype(np.float64)
    col = (t % GRID_W).astype(np.float64)
    n_pairs_axis = HEAD_DIM // 4
    inv_freq = ROPE_THETA ** (-np.arange(n_pairs_axis, dtype=np.float64) / n_pairs_axis)
    ang = np.concatenate([row[:, None] * inv_freq[None, :], col[:, None] * inv_freq[None, :]], axis=-1)
    cos = np.repeat(np.cos(ang), 2, axis=-1)
    sin = np.stack([-np.sin(ang), np.sin(ang)], axis=-1).reshape(n, HEAD_DIM)
    return cos.astype(np.float32), sin.astype(np.float32)


def kernel(x, c, ctx, c_ctx, w_mod, b_mod, w_in, q_norm_g, k_norm_g, hy_short_w, hy_short_b, hf_w1, hf_b1,
           hf_freq, hf_w2, hf_b2, hf_w3, hy_bias, nat_rpb, g_mix, w_out, ln1_g, ln1_b, w1, w2, ln2_g, ln2_b):
    bsz, n_lat, d = x.shape
    n_ctx = ctx.shape[1]
    depth = w_mod.shape[0]
    n_mix_heads = d // HEAD_DIM
    a_heads = n_mix_heads // 2
    hy = (n_mix_heads // 4) * HEAD_DIM
    c_heads = n_mix_heads - a_heads - hy // HEAD_DIM
    a_w, kv_w, c_w = a_heads * HEAD_DIM, A_KV_HEADS * HEAD_DIM, c_heads * HEAD_DIM
    attn_w = a_w + 2 * kv_w
    hy_col0 = attn_w
    nat_col0 = attn_w + 3 * hy
    assert w_in.shape[2] == nat_col0 + 3 * c_w and n_lat % GRID_W == 0
    alpha = (2 * depth) ** 0.25
    scale = HEAD_DIM ** -0.5 * LOG2E
    rows = n_lat // GRID_W
    kr = min((nat_rpb.shape[2] + 1) // 2, rows)
    ub_rows, ub_starts, ub_pids, nat_patterns = _nat_plan(rows, kr)

    n_rows = -(-(bsz + 1) // 8) * 8
    cc = jnp.zeros((n_rows, d), F32).at[:bsz].set(c).at[bsz].set(c_ctx)
    modtab = _modulation(cc, w_mod, b_mod).reshape(depth, n_rows, 1, 6 * d)
    lat_mod = lambda b: b
    ctx_mod = lambda b: bsz

    cos, sin = _rope_tables(n_lat)
    tabs_lat = _dft_tables(n_lat)
    spec_lat, mid_lat = _filter_spectrum(n_lat, hf_w1, hf_b1, hf_freq, hf_w2, hf_b2, hf_w3, tabs_lat,
                                         list(range(depth)))
    if depth > 1:
        tabs_ctx = _dft_tables(n_ctx)
        spec_ctx, mid_ctx = _filter_spectrum(n_ctx, hf_w1, hf_b1, hf_freq, hf_w2, hf_b2, hf_w3, tabs_ctx,
                                             list(range(depth - 1)))

    xl = x.reshape(bsz * n_lat, d)
    hc = ctx.reshape(bsz * n_ctx, d)
    for l in range(depth):
        keep_ctx = l < depth - 1
        w_in_l = _to_bf16(w_in, l)
        gq = q_norm_g[l].reshape(1, HEAD_DIM)
        gk = k_norm_g[l].reshape(1, HEAD_DIM)
        bias = _nat_bias(nat_rpb[l], kr, ub_rows, tuple(nat_patterns))

        proj_kw = dict(n_q=a_heads, n_k=A_KV_HEADS, plain_col0=hy_col0, nat_col0=nat_col0, n_nat_q=c_heads,
                       scale=scale)
        proj = _inproj(xl, modtab, l, n_lat, lat_mod, w_in_l, gq, gk, cos=cos, sin=sin, **proj_kw)
        cproj = _inproj(hc, modtab, l, n_ctx, ctx_mod, w_in_l, gq, gk, **proj_kw)

        o_a = _attention(proj, 0, a_heads, A_KV_HEADS,
                         [(proj, a_w, a_w + kv_w, n_lat), (cproj, a_w, a_w + kv_w, n_ctx)], bsz, n_lat)
        o_b = _hyena(proj, hy_col0, hy, hy_short_w[l], hy_short_b[l], spec_lat, mid_lat, l, hy_bias[l], tabs_lat,
                     bsz, n_lat)
        o_c = _natten(proj, cproj, nat_col0, bias, ub_starts, ub_pids, bsz, n_lat, n_ctx, c_heads)
        w_out_l = _to_bf16(w_out, l)
        xl_new, ul = _merge(o_a, o_b, o_c, xl, modtab, l, n_lat, lat_mod, g_mix[l], w_out_l, ln1_g[l], ln1_b[l], alpha)

        w1_l = _to_bf16(w1, l)
        w2_l = _to_bf16(w2, l)
        if keep_ctx:
            co_a = _attention(cproj, 0, a_heads, A_KV_HEADS, [(cproj, a_w, a_w + kv_w, n_ctx)], bsz, n_ctx)
            co_b = _hyena(cproj, hy_col0, hy, hy_short_w[l], hy_short_b[l], spec_ctx, mid_ctx, l, hy_bias[l],
                          tabs_ctx, bsz, n_ctx)
            co_c = _attention(cproj, nat_col0, c_heads, c_heads,
                              [(cproj, nat_col0 + c_w, nat_col0 + 2 * c_w, n_ctx)], bsz, n_ctx)
            hc, uc = _merge(co_a, co_b, co_c, hc, modtab, l, n_ctx, ctx_mod, g_mix[l], w_out_l, ln1_g[l], ln1_b[l], alpha)
            hc = _mlp(uc, hc, modtab, l, bsz * n_ctx, ctx_mod, w1_l, w2_l, ln2_g[l], ln2_b[l], alpha)
        xl = _mlp(ul, xl_new, modtab, l, n_lat, lat_mod, w1_l, w2_l, ln2_g[l], ln2_b[l], alpha)
    return xl.reshape(bsz, n_lat, d)
```
